```python
import math
import jax, jax.numpy as jnp
from jax import lax
import numpy as np

D_MODEL = 2048
BATCH = 2
SEQ = 8192
DEPTH = 4

N_FOURIER_GROUPS = 4
FOURIER_GROUP_DIM = D_MODEL // 8
D_FOURIER = N_FOURIER_GROUPS * FOURIER_GROUP_DIM
N_HEADS = 8
QK_HEAD_DIM = D_MODEL // 32
V_HEAD_DIM = 2 * QK_HEAD_DIM
D_QK = N_HEADS * 2 * QK_HEAD_DIM
D_V = N_HEADS * V_HEAD_DIM
ROPE_DIM = QK_HEAD_DIM // 4
ROPE_THETA = 500000.0
D_FF = ((8 * D_MODEL // 3 + 255) // 256) * 256
D_IN = D_FOURIER + 2 * D_QK + D_V + 2 * D_MODEL
Q_BLOCK = 128
EPS = 1e-6
LAMBDA_STD = 0.1

kernel_name = 'hybrid_fourier_diffattn_macaron_encoder'


def rms_norm(x, g):
    xf = x.astype(jnp.float32)
    y = xf * lax.rsqrt(jnp.mean(xf * xf, axis=-1, keepdims=True) + EPS)
    return (y * g.astype(jnp.float32)).astype(x.dtype)


def swiglu(h, w_gate, w_up, w_down):
    return (jax.nn.silu(h @ w_gate) * (h @ w_up)) @ w_down


def rope_tables(s):
    pos = jnp.arange(s, dtype=jnp.float32)
    inv_freq = ROPE_THETA ** (-jnp.arange(0, ROPE_DIM, 2, dtype=jnp.float32) / ROPE_DIM)
    ang = pos[:, None] * inv_freq[None, :]
    return jnp.cos(ang)[None, :, None, None, :], jnp.sin(ang)[None, :, None, None, :]


def partial_rope(x, cos, sin):
    half = ROPE_DIM // 2
    xr = x[..., :ROPE_DIM].astype(jnp.float32)
    x1, x2 = xr[..., :half], xr[..., half:]
    rot = jnp.concatenate([x1 * cos - x2 * sin, x2 * cos + x1 * sin], axis=-1)
    return jnp.concatenate([rot.astype(x.dtype), x[..., ROPE_DIM:]], axis=-1)


def fourier_mix(u):
    b, s, _ = u.shape
    ug = u.reshape(b, s, N_FOURIER_GROUPS, FOURIER_GROUP_DIM).astype(jnp.float32)
    f = jnp.fft.fft2(ug, axes=(1, 3), norm='ortho').real
    return f.reshape(b, s, D_FOURIER).astype(u.dtype)


def diff_attention(q, k, v, lam):
    b, s = q.shape[0], q.shape[1]
    nb = s // Q_BLOCK
    scale = QK_HEAD_DIM ** -0.5
    qb = jnp.moveaxis(q.reshape(b, nb, Q_BLOCK, N_HEADS, 2, QK_HEAD_DIM), 1, 0)

    def block(qi):
        sc = jnp.einsum('bqhcd,bkhcd->bhcqk', qi, k, preferred_element_type=jnp.float32) * scale
        p = jax.nn.softmax(sc, axis=-1)
        w = p[:, :, 0] - lam * p[:, :, 1]
        return jnp.einsum('bhqk,bkhe->bqhe', w.astype(v.dtype), v)

    o = lax.map(block, qb)
    return jnp.moveaxis(o, 0, 1).reshape(b, s, N_HEADS, V_HEAD_DIM)


def setup_inputs(seed: int = 0) -> dict:
    key = jax.random.key(seed)
    ks = jax.random.split(key, 24)

    def w(k, shape, fan_in):
        return jax.random.normal(k, shape, jnp.float32) * (fan_in ** -0.5)

    def gain(k, n):
        return 1.0 + 0.02 * jax.random.normal(k, (DEPTH, n), jnp.float32)

    return {
        'x': jax.random.normal(ks[0], (BATCH, SEQ, D_MODEL), jnp.float32),
        'norm_ffa': gain(ks[1], D_MODEL),
        'ffa_gate': w(ks[2], (DEPTH, D_MODEL, D_FF), D_MODEL),
        'ffa_up': w(ks[3], (DEPTH, D_MODEL, D_FF), D_MODEL),
        'ffa_down': w(ks[4], (DEPTH, D_FF, D_MODEL), D_FF),
        'norm_mix': gain(ks[5], D_MODEL),
        'w_in': w(ks[6], (DEPTH, D_MODEL, D_IN), D_MODEL),
        'q_norm': gain(ks[7], QK_HEAD_DIM),
        'k_norm': gain(ks[8], QK_HEAD_DIM),
        'lambda_q1': LAMBDA_STD * jax.random.normal(ks[9], (DEPTH, QK_HEAD_DIM), jnp.float32),
        'lambda_k1': LAMBDA_STD * jax.random.normal(ks[10], (DEPTH, QK_HEAD_DIM), jnp.float32),
        'lambda_q2': LAMBDA_STD * jax.random.normal(ks[11], (DEPTH, QK_HEAD_DIM), jnp.float32),
        'lambda_k2': LAMBDA_STD * jax.random.normal(ks[12], (DEPTH, QK_HEAD_DIM), jnp.float32),
        'subln': gain(ks[13], V_HEAD_DIM),
        'p_f': w(ks[14], (DEPTH, D_FOURIER, D_MODEL), D_FOURIER),
        'p_a': w(ks[15], (DEPTH, D_V, D_MODEL), D_V),
        'w_o': w(ks[16], (DEPTH, D_MODEL, D_MODEL), D_MODEL),
        'norm_ffb': gain(ks[17], D_MODEL),
        'ffb_gate': w(ks[18], (DEPTH, D_MODEL, D_FF), D_MODEL),
        'ffb_up': w(ks[19], (DEPTH, D_MODEL, D_FF), D_MODEL),
        'ffb_down': w(ks[20], (DEPTH, D_FF, D_MODEL), D_FF),
        'norm_out': gain(ks[21], D_MODEL),
    }


def reference(x, norm_ffa, ffa_gate, ffa_up, ffa_down, norm_mix, w_in, q_norm, k_norm,
              lambda_q1, lambda_k1, lambda_q2, lambda_k2, subln, p_f, p_a, w_o,
              norm_ffb, ffb_gate, ffb_up, ffb_down, norm_out):
    b, s, _ = x.shape
    cos, sin = rope_tables(s)
    splits = [D_FOURIER, D_FOURIER + D_QK, D_FOURIER + 2 * D_QK,
              D_FOURIER + 2 * D_QK + D_V, D_FOURIER + 2 * D_QK + D_V + D_MODEL]
    for i in range(DEPTH):
        lam_init = 0.8 - 0.6 * math.exp(-0.3 * i)
        x = x + 0.5 * swiglu(rms_norm(x, norm_ffa[i]), ffa_gate[i], ffa_up[i], ffa_down[i])
        h = rms_norm(x, norm_mix[i])
        z = h @ w_in[i]
        u_f, q, k, v, g_f, g_a = jnp.split(z, splits, axis=-1)
        f = fourier_mix(u_f)
        q = partial_rope(rms_norm(q.reshape(b, s, N_HEADS, 2, QK_HEAD_DIM), q_norm[i]), cos, sin)
        k = partial_rope(rms_norm(k.reshape(b, s, N_HEADS, 2, QK_HEAD_DIM), k_norm[i]), cos, sin)
        v = v.reshape(b, s, N_HEADS, V_HEAD_DIM)
        lam = (jnp.exp(jnp.sum(lambda_q1[i].astype(jnp.float32) * lambda_k1[i].astype(jnp.float32)))
               - jnp.exp(jnp.sum(lambda_q2[i].astype(jnp.float32) * lambda_k2[i].astype(jnp.float32)))
               + lam_init)
        o = diff_attention(q, k, v, lam)
        o = (rms_norm(o, subln[i]) * (1.0 - lam_init)).reshape(b, s, D_V)
        m = jax.nn.sigmoid(g_f) * (f @ p_f[i]) + jax.nn.sigmoid(g_a) * (o @ p_a[i])
        x = x + m @ w_o[i]
        x = x + 0.5 * swiglu(rms_norm(x, norm_ffb[i]), ffb_gate[i], ffb_up[i], ffb_down[i])
        x = rms_norm(x, norm_out[i])
    return x
```

```python
import functools
import math

import jax
import jax.numpy as jnp
from jax import lax
from jax.experimental import pallas as pl
from jax.experimental.pallas import tpu as pltpu

F32 = jnp.float32
BF16 = jnp.bfloat16

N_FOURIER_GROUPS = 4
ROPE_THETA = 500000.0
EPS = 1e-6
LOG2E = 1.4426950408889634
NEG_BIG = -1e30

V7X_LANES = 128
V7X_MXU_DIM = 256
V7X_VMEM_LIMIT_BYTES = 56 * 1024 * 1024


def _params(*semantics):
    return pltpu.CompilerParams(dimension_semantics=semantics,
                                vmem_limit_bytes=V7X_VMEM_LIMIT_BYTES)


def _rms_scale(x):
    return lax.rsqrt(jnp.mean(x * x, axis=-1, keepdims=True) + EPS)


def _ffn_kernel(x_ref, gin_ref, wg_ref, wu_ref, wd_ref, gout_ref, o_ref, h_ref, *, final_norm):
    j = pl.program_id(1)

    @pl.when(j == 0)
    def _():
        x = x_ref[...]
        h_ref[...] = (x * _rms_scale(x) * gin_ref[...]).astype(BF16)
        o_ref[...] = x

    h = h_ref[...]
    g = jnp.dot(h, wg_ref[...], preferred_element_type=F32)
    u = jnp.dot(h, wu_ref[...], preferred_element_type=F32)
    a = (0.5 * g) * jax.nn.sigmoid(g) * u
    o_ref[...] += jnp.dot(a.astype(BF16), wd_ref[...], preferred_element_type=F32)

    if final_norm:
        @pl.when(j == pl.num_programs(1) - 1)
        def _():
            y = o_ref[...]
            o_ref[...] = y * _rms_scale(y) * gout_ref[...]


def _ffn(x, g_in, wg, wu, wd, g_out, *, final_norm, tm, tf):
    m, d = x.shape
    f = wg.shape[1]
    return pl.pallas_call(
        functools.partial(_ffn_kernel, final_norm=final_norm),
        grid=(m // tm, f // tf),
        in_specs=[
            pl.BlockSpec((tm, d), lambda i, j: (i, 0)),
            pl.BlockSpec((1, d), lambda i, j: (0, 0)),
            pl.BlockSpec((d, tf), lambda i, j: (0, j)),
            pl.BlockSpec((d, tf), lambda i, j: (0, j)),
            pl.BlockSpec((tf, d), lambda i, j: (j, 0)),
            pl.BlockSpec((1, d), lambda i, j: (0, 0)),
        ],
        out_specs=pl.BlockSpec((tm, d), lambda i, j: (i, 0)),
        out_shape=jax.ShapeDtypeStruct((m, d), F32),
        scratch_shapes=[pltpu.VMEM((tm, d), BF16)],
        compiler_params=_params("parallel", "arbitrary"),
        name="ffn",
    )(x, g_in, wg, wu, wd, g_out)


def _norm_rope(z, e_ref, gain, rc, rs1, rs2, head_dim):
    tn = z.shape[1]
    zz = (z * z).astype(BF16)
    half = head_dim // 8
    outs = []
    for c in range(tn // V7X_MXU_DIM):
        cols = slice(c * V7X_MXU_DIM, (c + 1) * V7X_MXU_DIM)
        ss = jnp.dot(zz[:, cols], e_ref[...], preferred_element_type=F32)
        y = z[:, cols] * lax.rsqrt(ss * (1.0 / head_dim) + EPS) * gain[:, cols]
        for hh in range(V7X_MXU_DIM // V7X_LANES):
            yc = y[:, hh * V7X_LANES:(hh + 1) * V7X_LANES]
            outs.append(yc * rc + pltpu.roll(yc, V7X_LANES - half, 1) * rs1
                        + pltpu.roll(yc, half, 1) * rs2)
    return jnp.concatenate(outs, axis=1)


def _inproj_kernel(x_ref, gn_ref, w_ref, wc_ref, e_ref, qg_ref, kg_ref, rc_ref, rs1_ref, rs2_ref,
                   ab_ref, qT_ref, k_ref, vT_ref, sg_ref, h_ref, *, group_dim, head_dim):
    j = pl.program_id(1)

    @pl.when(j == 0)
    def _():
        x = x_ref[...]
        h_ref[...] = (x * _rms_scale(x) * gn_ref[...]).astype(BF16)

    z = jnp.dot(h_ref[...], w_ref[...], preferred_element_type=F32)

    @pl.when(j == 0)
    def _():
        u = z.astype(BF16)
        n_groups = z.shape[1] // group_dim
        for g in range(n_groups):
            t = jnp.dot(u[:, g * group_dim:(g + 1) * group_dim], wc_ref[...],
                        preferred_element_type=F32)
            ab_ref[:, g * group_dim:(g + 1) * group_dim] = t[:, :group_dim].astype(BF16)
            ab_ref[:, (n_groups + g) * group_dim:(n_groups + g + 1) * group_dim] = (
                t[:, group_dim:].astype(BF16))

    @pl.when(j == 1)
    def _():
        y = _norm_rope(z, e_ref, qg_ref[...], rc_ref[...], rs1_ref[...], rs2_ref[...], head_dim)
        qT_ref[0] = y.T.astype(BF16)

    @pl.when(j == 2)
    def _():
        y = _norm_rope(z, e_ref, kg_ref[...], rc_ref[...], rs1_ref[...], rs2_ref[...], head_dim)
        k_ref[...] = y.astype(BF16)

    @pl.when(j == 3)
    def _():
        vT_ref[0] = z.T.astype(BF16)

    @pl.when(j >= 4)
    def _():
        sg_ref[...] = jax.nn.sigmoid(z).astype(BF16)


def _inproj(x, g_norm, w_in, wc, e, qg, kg, rc, rs1, rs2, *, batch, seq, tm, tn, group_dim, head_dim):
    m, d = x.shape
    n_blocks = w_in.shape[1] // tn
    n_gate_blocks = n_blocks - 4
    n_s = seq // tm
    const = lambda i, j: (0, 0)
    return pl.pallas_call(
        functools.partial(_inproj_kernel, group_dim=group_dim, head_dim=head_dim),
        grid=(m // tm, n_blocks),
        in_specs=[
            pl.BlockSpec((tm, d), lambda i, j: (i, 0)),
            pl.BlockSpec((1, d), const),
            pl.BlockSpec((d, tn), lambda i, j: (0, j)),
            pl.BlockSpec(wc.shape, const),
            pl.BlockSpec(e.shape, const),
            pl.BlockSpec((1, tn), const),
            pl.BlockSpec((1, tn), const),
            pl.BlockSpec((tm, V7X_LANES), lambda i, j: (i % n_s, 0)),
            pl.BlockSpec((tm, V7X_LANES), lambda i, j: (i % n_s, 0)),
            pl.BlockSpec((tm, V7X_LANES), lambda i, j: (i % n_s, 0)),
        ],
        out_specs=[
            pl.BlockSpec((tm, 2 * tn), lambda i, j: (i, 0)),
            pl.BlockSpec((1, tn, tm), lambda i, j: (i // n_s, 0, i % n_s)),
            pl.BlockSpec((tm, tn), lambda i, j: (i, 0)),
            pl.BlockSpec((1, tn, tm), lambda i, j: (i // n_s, 0, i % n_s)),
            pl.BlockSpec((tm, tn), lambda i, j: (i, jnp.maximum(j - 4, 0))),
        ],
        out_shape=[
            jax.ShapeDtypeStruct((m, 2 * tn), BF16),
            jax.ShapeDtypeStruct((batch, tn, seq), BF16),
            jax.ShapeDtypeStruct((m, tn), BF16),
            jax.ShapeDtypeStruct((batch, tn, seq), BF16),
            jax.ShapeDtypeStruct((m, n_gate_blocks * tn), BF16),
        ],
        scratch_shapes=[pltpu.VMEM((tm, d), BF16)],
        compiler_params=_params("parallel", "arbitrary"),
        name="inproj",
    )(x, g_norm, w_in, wc, e, qg, kg, rc, rs1, rs2)


def _fourier_kernel(c_ref, s_ref, a_ref, b_ref, o_ref, acc_ref):
    kk = pl.program_id(1)

    @pl.when(kk == 0)
    def _():
        acc_ref[...] = jnp.zeros_like(acc_ref)

    for b in range(a_ref.shape[0]):
        acc_ref[b] += (jnp.dot(c_ref[...], a_ref[b], preferred_element_type=F32)
                       + jnp.dot(s_ref[...], b_ref[b], preferred_element_type=F32))

    @pl.when(kk == pl.num_programs(1) - 1)
    def _():
        o_ref[...] = acc_ref[...].astype(o_ref.dtype)


def _fourier(cmat, smat, ab, *, tm, tk):
    batch, seq, two_w = ab.shape
    w = two_w // 2
    return pl.pallas_call(
        _fourier_kernel,
        grid=(seq // tm, seq // tk),
        in_specs=[
            pl.BlockSpec((tm, tk), lambda i, kk: (i, kk)),
            pl.BlockSpec((tm, tk), lambda i, kk: (i, kk)),
            pl.BlockSpec((batch, tk, w), lambda i, kk: (0, kk, 0)),
            pl.BlockSpec((batch, tk, w), lambda i, kk: (0, kk, 1)),
        ],
        out_specs=pl.BlockSpec((batch, tm, w), lambda i, kk: (0, i, 0)),
        out_shape=jax.ShapeDtypeStruct((batch, seq, w), BF16),
        scratch_shapes=[pltpu.VMEM((batch, tm, w), F32)],
        compiler_params=_params("parallel", "arbitrary"),
        name="fourier",
    )(cmat, smat, ab, ab)


def _attn_kernel(qT_ref, k_ref, vT_ref, lq1_ref, lk1_ref, lq2_ref, lk2_ref, sub_ref, o_ref,
                 m_ref, l_ref, acc_ref, *, tk, lam_init):
    qT = qT_ref[0]
    qk_dim = qT.shape[0] // 2
    row = lax.broadcasted_iota(jnp.int32, qT.shape, 0)
    zero = jnp.zeros_like(qT)
    q_comp = (jnp.where(row < qk_dim, qT, zero), jnp.where(row >= qk_dim, qT, zero))

    m_ref[...] = jnp.full_like(m_ref, NEG_BIG)
    l_ref[...] = jnp.zeros_like(l_ref)
    acc_ref[...] = jnp.zeros_like(acc_ref)

    def body(j, carry):
        start = pl.multiple_of(j * tk, tk)
        kb = k_ref[0, pl.ds(start, tk), :]
        vb = vT_ref[0, :, pl.ds(start, tk)]
        for c in range(2):
            s = jnp.dot(kb, q_comp[c], preferred_element_type=F32)
            m_prev = m_ref[c]
            m_new = jnp.maximum(m_prev, jnp.max(s, axis=0, keepdims=True))
            alpha = jnp.exp2(m_prev - m_new)
            p = jnp.exp2(s - m_new)
            l_ref[c] = alpha * l_ref[c] + jnp.sum(p, axis=0, keepdims=True)
            acc_ref[c] = acc_ref[c] * alpha + jnp.dot(vb, p.astype(BF16),
                                                      preferred_element_type=F32)
            m_ref[c] = m_new
        return carry

    lax.fori_loop(0, k_ref.shape[1] // tk, body, 0)

    lam = (jnp.exp(jnp.sum(lq1_ref[...] * lk1_ref[...], axis=-1, keepdims=True))
           - jnp.exp(jnp.sum(lq2_ref[...] * lk2_ref[...], axis=-1, keepdims=True)) + lam_init)
    o = acc_ref[0] * (1.0 / l_ref[0]) - acc_ref[1] * (lam / l_ref[1])
    scale = lax.rsqrt(jnp.mean(o * o, axis=0, keepdims=True) + EPS)
    y = o * scale * (sub_ref[...] * (1.0 - lam_init))
    o_ref[0] = y.T.astype(o_ref.dtype)


def _attention(qT, k, vT, lq1, lk1, lq2, lk2, sub, *, lam_init, n_heads, tq, tk):
    batch, d_qk, seq = qT.shape
    d_v = vT.shape[1]
    hq, hv = d_qk // n_heads, d_v // n_heads
    vec = lambda b, h, i: (0, 0)
    return pl.pallas_call(
        functools.partial(_attn_kernel, tk=tk, lam_init=lam_init),
        grid=(batch, n_heads, seq // tq),
        in_specs=[
            pl.BlockSpec((1, hq, tq), lambda b, h, i: (b, h, i)),
            pl.BlockSpec((1, seq, hq), lambda b, h, i: (b, 0, h)),
            pl.BlockSpec((1, hv, seq), lambda b, h, i: (b, h, 0)),
            pl.BlockSpec(lq1.shape, vec),
            pl.BlockSpec(lk1.shape, vec),
            pl.BlockSpec(lq2.shape, vec),
            pl.BlockSpec(lk2.shape, vec),
            pl.BlockSpec(sub.shape, vec),
        ],
        out_specs=pl.BlockSpec((1, tq, hv), lambda b, h, i: (b, i, h)),
        out_shape=jax.ShapeDtypeStruct((batch, seq, d_v), BF16),
        scratch_shapes=[
            pltpu.VMEM((2, 1, tq), F32),
            pltpu.VMEM((2, 1, tq), F32),
            pltpu.VMEM((2, hv, tq), F32),
        ],
        compiler_params=_params("parallel", "parallel", "arbitrary"),
        name="attn",
    )(qT, k, vT, lq1, lk1, lq2, lk2, sub)


def _merge_kernel(x_ref, f_ref, o_ref, sg_ref, pf_ref, pa_ref, wo_ref, out_ref):
    d = x_ref.shape[1]
    bf = jnp.dot(f_ref[...], pf_ref[...], preferred_element_type=F32)
    ba = jnp.dot(o_ref[...], pa_ref[...], preferred_element_type=F32)
    mix = sg_ref[:, :d].astype(F32) * bf + sg_ref[:, d:].astype(F32) * ba
    out_ref[...] = x_ref[...] + jnp.dot(mix.astype(BF16), wo_ref[...], preferred_element_type=F32)


def _merge(x, f, o, sg, p_f, p_a, w_o, *, tm):
    m, d = x.shape
    row = lambda i: (i, 0)
    const = lambda i: (0, 0)
    return pl.pallas_call(
        _merge_kernel,
        grid=(m // tm,),
        in_specs=[
            pl.BlockSpec((tm, d), row),
            pl.BlockSpec((tm, f.shape[1]), row),
            pl.BlockSpec((tm, o.shape[1]), row),
            pl.BlockSpec((tm, sg.shape[1]), row),
            pl.BlockSpec(p_f.shape, const),
            pl.BlockSpec(p_a.shape, const),
            pl.BlockSpec(w_o.shape, const),
        ],
        out_specs=pl.BlockSpec((tm, d), row),
        out_shape=jax.ShapeDtypeStruct((m, d), F32),
        compiler_params=_params("parallel"),
        name="merge",
    )(x, f, o, sg, p_f, p_a, w_o)


def _dft_tables(seq, group_dim):
    n = jnp.arange(seq, dtype=jnp.int32)
    ang = ((n[:, None] * n[None, :]) % seq).astype(F32) * (2.0 * math.pi / seq)
    cmat = jnp.cos(ang).astype(BF16)
    smat = (-jnp.sin(ang)).astype(BF16)
    c = jnp.arange(group_dim, dtype=jnp.int32)
    angc = ((c[:, None] * c[None, :]) % group_dim).astype(F32) * (2.0 * math.pi / group_dim)
    ortho = 1.0 / math.sqrt(seq * group_dim)
    wc = (jnp.concatenate([jnp.cos(angc), jnp.sin(angc)], axis=1) * ortho).astype(BF16)
    return cmat, smat, wc


def _rope_lane_tables(seq, head_dim):
    rope_dim = head_dim // 4
    half = rope_dim // 2
    pos = jnp.arange(seq, dtype=F32)
    inv_freq = ROPE_THETA ** (-jnp.arange(0, rope_dim, 2, dtype=F32) / rope_dim)
    ang = pos[:, None] * inv_freq[None, :]
    cos, sin = jnp.cos(ang), jnp.sin(ang)
    ones = jnp.ones((seq, head_dim - rope_dim), F32)
    zeros_h = jnp.zeros((seq, half), F32)
    zeros_r = jnp.zeros((seq, head_dim - rope_dim), F32)
    rc = jnp.concatenate([cos, cos, ones], axis=1)
    rs1 = jnp.concatenate([-sin, zeros_h, zeros_r], axis=1)
    rs2 = jnp.concatenate([zeros_h, sin, zeros_r], axis=1)
    reps = V7X_LANES // head_dim
    return tuple(jnp.tile(t, (1, reps)) for t in (rc, rs1, rs2))


def kernel(x, norm_ffa, ffa_gate, ffa_up, ffa_down, norm_mix, w_in, q_norm, k_norm, lambda_q1,
           lambda_k1, lambda_q2, lambda_k2, subln, p_f, p_a, w_o, norm_ffb, ffb_gate, ffb_up,
           ffb_down, norm_out):
    batch, seq, d = x.shape
    depth = w_in.shape[0]
    head_dim = q_norm.shape[1]
    v_dim = subln.shape[1]
    d_f = p_f.shape[1]
    d_v = p_a.shape[1]
    n_heads = d_v // v_dim
    group_dim = d_f // N_FOURIER_GROUPS
    tn = d_f
    assert w_in.shape[2] == 4 * tn + 2 * d and d_v == tn and n_heads * 2 * head_dim == tn
    assert 2 * head_dim == V7X_LANES and v_dim == V7X_LANES

    m = batch * seq
    tm_ffn = min(512, seq)
    tm_in = min(512, seq)
    tm_merge = min(256, seq)
    t_dft = min(1024, seq)
    tq = min(256, seq)
    tk = min(512, seq)
    tf = 512

    cmat, smat, wc = _dft_tables(seq, group_dim)
    rc, rs1, rs2 = _rope_lane_tables(seq, head_dim)
    blk = jnp.arange(V7X_MXU_DIM, dtype=jnp.int32) // head_dim
    e = (blk[:, None] == blk[None, :]).astype(BF16)
    q_scale = head_dim ** -0.5 * LOG2E

    bf = lambda w: w.astype(BF16)
    xs = x.reshape(m, d)
    for i in range(depth):
        lam_init = 0.8 - 0.6 * math.exp(-0.3 * i)
        xs = _ffn(xs, norm_ffa[i][None], bf(ffa_gate[i]), bf(ffa_up[i]), bf(ffa_down[i]),
                  norm_ffa[i][None], final_norm=False, tm=tm_ffn, tf=tf)
        qg = jnp.tile(q_norm[i], tn // head_dim)[None] * q_scale
        kg = jnp.tile(k_norm[i], tn // head_dim)[None]
        ab, qT, k, vT, sg = _inproj(xs, norm_mix[i][None], bf(w_in[i]), wc, e, qg, kg, rc, rs1, rs2,
                                    batch=batch, seq=seq, tm=tm_in, tn=tn, group_dim=group_dim,
                                    head_dim=head_dim)
        f = _fourier(cmat, smat, ab.reshape(batch, seq, 2 * tn), tm=t_dft, tk=t_dft)
        o = _attention(qT, k.reshape(batch, seq, tn), vT, lambda_q1[i][None], lambda_k1[i][None],
                       lambda_q2[i][None], lambda_k2[i][None], subln[i][:, None],
                       lam_init=lam_init, n_heads=n_heads, tq=tq, tk=tk)
        xs = _merge(xs, f.reshape(m, d_f), o.reshape(m, d_v), sg, bf(p_f[i]), bf(p_a[i]), bf(w_o[i]),
                    tm=tm_merge)
        xs = _ffn(xs, norm_ffb[i][None], bf(ffb_gate[i]), bf(ffb_up[i]), bf(ffb_down[i]),
                  norm_out[i][None], final_norm=True, tm=tm_ffn, tf=tf)
    return xs.reshape(batch, seq, d)
```

```python
import functools
import math

import jax
import jax.numpy as jnp
from jax import lax
from jax.experimental import pallas as pl
from jax.experimental.pallas import tpu as pltpu

F32 = jnp.float32
BF16 = jnp.bfloat16

N_FOURIER_GROUPS = 4
ROPE_THETA = 500000.0
EPS = 1e-6
LOG2E = 1.4426950408889634
NEG_BIG = -1e30
SCORE_BOUND_LIMIT = 64.0

V7X_LANES = 128
V7X_MXU_DIM = 256
V7X_VMEM_LIMIT_BYTES = 56 * 1024 * 1024


def _params(*semantics):
    return pltpu.CompilerParams(dimension_semantics=semantics,
                                vmem_limit_bytes=V7X_VMEM_LIMIT_BYTES)


def _rms_scale(x):
    return lax.rsqrt(jnp.mean(x * x, axis=-1, keepdims=True) + EPS)


def _ffn_kernel(x_ref, gin_ref, wg_ref, wu_ref, wd_ref, gout_ref, o_ref, h_ref, *, final_norm):
    j = pl.program_id(1)

    @pl.when(j == 0)
    def _():
        x = x_ref[...]
        h_ref[...] = (x * _rms_scale(x) * gin_ref[...]).astype(BF16)
        o_ref[...] = x

    h = h_ref[...]
    g = jnp.dot(h, wg_ref[...], preferred_element_type=F32)
    u = jnp.dot(h, wu_ref[...], preferred_element_type=F32)
    a = (0.5 * g) * jax.nn.sigmoid(g) * u
    o_ref[...] += jnp.dot(a.astype(BF16), wd_ref[...], preferred_element_type=F32)

    if final_norm:
        @pl.when(j == pl.num_programs(1) - 1)
        def _():
            y = o_ref[...]
            o_ref[...] = y * _rms_scale(y) * gout_ref[...]


def _ffn(x, g_in, wg, wu, wd, g_out, *, final_norm, tm, tf):
    m, d = x.shape
    f = wg.shape[1]
    return pl.pallas_call(
        functools.partial(_ffn_kernel, final_norm=final_norm),
        grid=(m // tm, f // tf),
        in_specs=[
            pl.BlockSpec((tm, d), lambda i, j: (i, 0), pipeline_mode=pl.Buffered(1)),
            pl.BlockSpec((1, d), lambda i, j: (0, 0)),
            pl.BlockSpec((d, tf), lambda i, j: (0, j)),
            pl.BlockSpec((d, tf), lambda i, j: (0, j)),
            pl.BlockSpec((tf, d), lambda i, j: (j, 0)),
            pl.BlockSpec((1, d), lambda i, j: (0, 0)),
        ],
        out_specs=pl.BlockSpec((tm, d), lambda i, j: (i, 0)),
        out_shape=jax.ShapeDtypeStruct((m, d), F32),
        scratch_shapes=[pltpu.VMEM((tm, d), BF16)],
        compiler_params=_params("parallel", "arbitrary"),
        name="ffn",
    )(x, g_in, wg, wu, wd, g_out)


def _norm_rope(z, e_ref, gain, rc, rs1, rs2, head_dim):
    tn = z.shape[1]
    zz = (z * z).astype(BF16)
    half = head_dim // 8
    outs = []
    for c in range(tn // V7X_MXU_DIM):
        cols = slice(c * V7X_MXU_DIM, (c + 1) * V7X_MXU_DIM)
        ss = jnp.dot(zz[:, cols], e_ref[...], preferred_element_type=F32)
        y = z[:, cols] * lax.rsqrt(ss * (1.0 / head_dim) + EPS) * gain[:, cols]
        for hh in range(V7X_MXU_DIM // V7X_LANES):
            yc = y[:, hh * V7X_LANES:(hh + 1) * V7X_LANES]
            outs.append(yc * rc + pltpu.roll(yc, V7X_LANES - half, 1) * rs1
                        + pltpu.roll(yc, half, 1) * rs2)
    return jnp.concatenate(outs, axis=1)


def _inproj_kernel(x_ref, gn_ref, w_ref, wc_ref, e_ref, qg_ref, kg_ref, rc_ref, rs1_ref, rs2_ref,
                   ab_ref, qT_ref, k_ref, vT_ref, sg_ref, h_ref, *, group_dim, head_dim):
    j = pl.program_id(1)

    @pl.when(j == 0)
    def _():
        x = x_ref[...]
        h_ref[...] = (x * _rms_scale(x) * gn_ref[...]).astype(BF16)

    z = jnp.dot(h_ref[...], w_ref[...], preferred_element_type=F32)

    @pl.when(j == 0)
    def _():
        u = z.astype(BF16)
        n_groups = z.shape[1] // group_dim
        for g in range(n_groups):
            t = jnp.dot(u[:, g * group_dim:(g + 1) * group_dim], wc_ref[...],
                        preferred_element_type=F32)
            ab_ref[:, g * group_dim:(g + 1) * group_dim] = t[:, :group_dim].astype(BF16)
            ab_ref[:, (n_groups + g) * group_dim:(n_groups + g + 1) * group_dim] = (
                t[:, group_dim:].astype(BF16))

    @pl.when(j == 1)
    def _():
        y = _norm_rope(z, e_ref, qg_ref[...], rc_ref[...], rs1_ref[...], rs2_ref[...], head_dim)
        qT_ref[0] = y.T.astype(BF16)

    @pl.when(j == 2)
    def _():
        y = _norm_rope(z, e_ref, kg_ref[...], rc_ref[...], rs1_ref[...], rs2_ref[...], head_dim)
        k_ref[...] = y.astype(BF16)

    @pl.when(j == 3)
    def _():
        vT_ref[0] = z.T.astype(BF16)

    @pl.when(j >= 4)
    def _():
        sg_ref[...] = jax.nn.sigmoid(z).astype(BF16)


def _inproj(x, g_norm, w_in, wc, e, qg, kg, rc, rs1, rs2, *, batch, seq, tm, tn, group_dim, head_dim):
    m, d = x.shape
    n_blocks = w_in.shape[1] // tn
    n_gate_blocks = n_blocks - 4
    n_s = seq // tm
    const = lambda i, j: (0, 0)
    return pl.pallas_call(
        functools.partial(_inproj_kernel, group_dim=group_dim, head_dim=head_dim),
        grid=(m // tm, n_blocks),
        in_specs=[
            pl.BlockSpec((tm, d), lambda i, j: (i, 0)),
            pl.BlockSpec((1, d), const),
            pl.BlockSpec((d, tn), lambda i, j: (0, j)),
            pl.BlockSpec(wc.shape, const),
            pl.BlockSpec(e.shape, const),
            pl.BlockSpec((1, tn), const),
            pl.BlockSpec((1, tn), const),
            pl.BlockSpec((tm, V7X_LANES), lambda i, j: (i % n_s, 0)),
            pl.BlockSpec((tm, V7X_LANES), lambda i, j: (i % n_s, 0)),
            pl.BlockSpec((tm, V7X_LANES), lambda i, j: (i % n_s, 0)),
        ],
        out_specs=[
            pl.BlockSpec((tm, 2 * tn), lambda i, j: (i, 0)),
            pl.BlockSpec((1, tn, tm), lambda i, j: (i // n_s, 0, i % n_s)),
            pl.BlockSpec((tm, tn), lambda i, j: (i, 0)),
            pl.BlockSpec((1, tn, tm), lambda i, j: (i // n_s, 0, i % n_s)),
            pl.BlockSpec((tm, tn), lambda i, j: (i, jnp.maximum(j - 4, 0))),
        ],
        out_shape=[
            jax.ShapeDtypeStruct((m, 2 * tn), BF16),
            jax.ShapeDtypeStruct((batch, tn, seq), BF16),
            jax.ShapeDtypeStruct((m, tn), BF16),
            jax.ShapeDtypeStruct((batch, tn, seq), BF16),
            jax.ShapeDtypeStruct((m, n_gate_blocks * tn), BF16),
        ],
        scratch_shapes=[pltpu.VMEM((tm, d), BF16)],
        compiler_params=_params("parallel", "arbitrary"),
        name="inproj",
    )(x, g_norm, w_in, wc, e, qg, kg, rc, rs1, rs2)


def _fourier_kernel(c_ref, s_ref, a_ref, b_ref, o_ref, acc_ref):
    kk = pl.program_id(1)

    @pl.when(kk == 0)
    def _():
        acc_ref[...] = jnp.zeros_like(acc_ref)

    for b in range(a_ref.shape[0]):
        acc_ref[b] += (jnp.dot(c_ref[...], a_ref[b], preferred_element_type=F32)
                       + jnp.dot(s_ref[...], b_ref[b], preferred_element_type=F32))

    @pl.when(kk == pl.num_programs(1) - 1)
    def _():
        o_ref[...] = acc_ref[...].astype(o_ref.dtype)


def _fourier(cmat, smat, ab, *, tm, tk):
    batch, seq, two_w = ab.shape
    w = two_w // 2
    return pl.pallas_call(
        _fourier_kernel,
        grid=(seq // tm, seq // tk),
        in_specs=[
            pl.BlockSpec((tm, tk), lambda i, kk: (i, kk)),
            pl.BlockSpec((tm, tk), lambda i, kk: (i, kk)),
            pl.BlockSpec((batch, tk, w), lambda i, kk: (0, kk, 0)),
            pl.BlockSpec((batch, tk, w), lambda i, kk: (0, kk, 1)),
        ],
        out_specs=pl.BlockSpec((batch, tm, w), lambda i, kk: (0, i, 0)),
        out_shape=jax.ShapeDtypeStruct((batch, seq, w), BF16),
        scratch_shapes=[pltpu.VMEM((batch, tm, w), F32)],
        compiler_params=_params("parallel", "arbitrary"),
        name="fourier",
    )(cmat, smat, ab, ab)


def _attn_kernel(bounded_ref, qT_ref, k_ref, vT_ref, lq1_ref, lk1_ref, lq2_ref, lk2_ref, sub_ref,
                 o_ref, m_ref, l_ref, acc_ref, *, tk, lam_init):
    qT = qT_ref[0]
    qk_dim = qT.shape[0] // 2
    tq = qT.shape[1]
    row = lax.broadcasted_iota(jnp.int32, qT.shape, 0)
    zero = jnp.zeros_like(qT)
    q2 = jnp.concatenate([jnp.where(row < qk_dim, qT, zero), jnp.where(row >= qk_dim, qT, zero)],
                         axis=1)
    n_kb = k_ref.shape[1] // tk

    l_ref[...] = jnp.zeros_like(l_ref)
    acc_ref[...] = jnp.zeros_like(acc_ref)

    def blocks(j):
        start = pl.multiple_of(j * tk, tk)
        return k_ref[0, pl.ds(start, tk), :], vT_ref[0, :, pl.ds(start, tk)]

    @pl.when(bounded_ref[0] == 1)
    def _():
        def body(j, carry):
            kb, vb = blocks(j)
            s = jnp.dot(kb, q2, preferred_element_type=F32)
            p = jnp.exp2(s)
            l_ref[...] += jnp.sum(p.reshape(tk // 8, 8, 2 * tq), axis=0)
            acc_ref[...] += jnp.dot(vb, p.astype(BF16), preferred_element_type=F32)
            return carry

        lax.fori_loop(0, n_kb, body, 0)

    @pl.when(bounded_ref[0] != 1)
    def _():
        m_ref[...] = jnp.full_like(m_ref, NEG_BIG)

        def body(j, carry):
            kb, vb = blocks(j)
            s = jnp.dot(kb, q2, preferred_element_type=F32)
            m_prev = m_ref[...]
            m_new = jnp.maximum(m_prev, jnp.max(s, axis=0, keepdims=True))
            alpha = jnp.exp2(m_prev - m_new)
            p = jnp.exp2(s - m_new)
            l_ref[...] = alpha * l_ref[...] + jnp.sum(p.reshape(tk // 8, 8, 2 * tq), axis=0)
            acc_ref[...] = acc_ref[...] * alpha + jnp.dot(vb, p.astype(BF16),
                                                          preferred_element_type=F32)
            m_ref[...] = m_new
            return carry

        lax.fori_loop(0, n_kb, body, 0)

    lam = (jnp.exp(jnp.sum(lq1_ref[...] * lk1_ref[...], axis=-1, keepdims=True))
           - jnp.exp(jnp.sum(lq2_ref[...] * lk2_ref[...], axis=-1, keepdims=True)) + lam_init)
    l = jnp.sum(l_ref[...], axis=0, keepdims=True)
    o = acc_ref[:, :tq] * (1.0 / l[:, :tq]) - acc_ref[:, tq:] * (lam / l[:, tq:])
    scale = lax.rsqrt(jnp.mean(o * o, axis=0, keepdims=True) + EPS)
    y = o * scale * (sub_ref[...] * (1.0 - lam_init))
    o_ref[0] = y.T.astype(o_ref.dtype)


def _attention(bounded, qT, k, vT, lq1, lk1, lq2, lk2, sub, *, lam_init, n_heads, tq, tk):
    batch, d_qk, seq = qT.shape
    d_v = vT.shape[1]
    hq, hv = d_qk // n_heads, d_v // n_heads
    vec = lambda b, h, i, flag: (0, 0)
    return pl.pallas_call(
        functools.partial(_attn_kernel, tk=tk, lam_init=lam_init),
        grid_spec=pltpu.PrefetchScalarGridSpec(
            num_scalar_prefetch=1,
            grid=(batch, n_heads, seq // tq),
            in_specs=[
                pl.BlockSpec((1, hq, tq), lambda b, h, i, flag: (b, h, i)),
                pl.BlockSpec((1, seq, hq), lambda b, h, i, flag: (b, 0, h)),
                pl.BlockSpec((1, hv, seq), lambda b, h, i, flag: (b, h, 0)),
                pl.BlockSpec(lq1.shape, vec),
                pl.BlockSpec(lk1.shape, vec),
                pl.BlockSpec(lq2.shape, vec),
                pl.BlockSpec(lk2.shape, vec),
                pl.BlockSpec(sub.shape, vec),
            ],
            out_specs=pl.BlockSpec((1, tq, hv), lambda b, h, i, flag: (b, i, h)),
            scratch_shapes=[
                pltpu.VMEM((1, 2 * tq), F32),
                pltpu.VMEM((8, 2 * tq), F32),
                pltpu.VMEM((hv, 2 * tq), F32),
            ],
        ),
        out_shape=jax.ShapeDtypeStruct((batch, seq, d_v), BF16),
        compiler_params=_params("parallel", "parallel", "arbitrary"),
        name="attn",
    )(bounded, qT, k, vT, lq1, lk1, lq2, lk2, sub)


def _merge_kernel(x_ref, f_ref, o_ref, sg_ref, pf_ref, pa_ref, wo_ref, out_ref):
    d = x_ref.shape[1]
    bf = jnp.dot(f_ref[...], pf_ref[...], preferred_element_type=F32)
    ba = jnp.dot(o_ref[...], pa_ref[...], preferred_element_type=F32)
    mix = sg_ref[:, :d].astype(F32) * bf + sg_ref[:, d:].astype(F32) * ba
    out_ref[...] = x_ref[...] + jnp.dot(mix.astype(BF16), wo_ref[...], preferred_element_type=F32)


def _merge(x, f, o, sg, p_f, p_a, w_o, *, tm):
    m, d = x.shape
    row = lambda i: (i, 0)
    const = lambda i: (0, 0)
    return pl.pallas_call(
        _merge_kernel,
        grid=(m // tm,),
        in_specs=[
            pl.BlockSpec((tm, d), row),
            pl.BlockSpec((tm, f.shape[1]), row),
            pl.BlockSpec((tm, o.shape[1]), row),
            pl.BlockSpec((tm, sg.shape[1]), row),
            pl.BlockSpec(p_f.shape, const),
            pl.BlockSpec(p_a.shape, const),
            pl.BlockSpec(w_o.shape, const),
        ],
        out_specs=pl.BlockSpec((tm, d), row),
        out_shape=jax.ShapeDtypeStruct((m, d), F32),
        compiler_params=_params("parallel"),
        name="merge",
    )(x, f, o, sg, p_f, p_a, w_o)


def _dft_tables(seq, group_dim):
    k = jnp.arange(seq, dtype=jnp.int32)[:, None]
    hi = jnp.arange(seq // V7X_LANES, dtype=jnp.int32)[None, :]
    lo = jnp.arange(V7X_LANES, dtype=jnp.int32)[None, :]
    ang_hi = ((k * hi * V7X_LANES) % seq).astype(F32) * (2.0 * math.pi / seq)
    ang_lo = ((k * lo) % seq).astype(F32) * (2.0 * math.pi / seq)
    ca, sa = jnp.cos(ang_hi)[:, :, None], jnp.sin(ang_hi)[:, :, None]
    cb, sb = jnp.cos(ang_lo)[:, None, :], jnp.sin(ang_lo)[:, None, :]
    cmat = (ca * cb - sa * sb).astype(BF16).reshape(seq, seq)
    smat = (-(sa * cb + ca * sb)).astype(BF16).reshape(seq, seq)
    c = jnp.arange(group_dim, dtype=jnp.int32)
    angc = ((c[:, None] * c[None, :]) % group_dim).astype(F32) * (2.0 * math.pi / group_dim)
    ortho = 1.0 / math.sqrt(seq * group_dim)
    wc = (jnp.concatenate([jnp.cos(angc), jnp.sin(angc)], axis=1) * ortho).astype(BF16)
    return cmat, smat, wc


def _rope_lane_tables(seq, head_dim):
    rope_dim = head_dim // 4
    half = rope_dim // 2
    pos = jnp.arange(seq, dtype=F32)
    inv_freq = ROPE_THETA ** (-jnp.arange(0, rope_dim, 2, dtype=F32) / rope_dim)
    ang = pos[:, None] * inv_freq[None, :]
    cos, sin = jnp.cos(ang), jnp.sin(ang)
    ones = jnp.ones((seq, head_dim - rope_dim), F32)
    zeros_h = jnp.zeros((seq, half), F32)
    zeros_r = jnp.zeros((seq, head_dim - rope_dim), F32)
    rc = jnp.concatenate([cos, cos, ones], axis=1)
    rs1 = jnp.concatenate([-sin, zeros_h, zeros_r], axis=1)
    rs2 = jnp.concatenate([zeros_h, sin, zeros_r], axis=1)
    reps = V7X_LANES // head_dim
    return tuple(jnp.tile(t, (1, reps)) for t in (rc, rs1, rs2))


def kernel(x, norm_ffa, ffa_gate, ffa_up, ffa_down, norm_mix, w_in, q_norm, k_norm, lambda_q1,
           lambda_k1, lambda_q2, lambda_k2, subln, p_f, p_a, w_o, norm_ffb, ffb_gate, ffb_up,
           ffb_down, norm_out):
    batch, seq, d = x.shape
    depth = w_in.shape[0]
    head_dim = q_norm.shape[1]
    v_dim = subln.shape[1]
    d_f = p_f.shape[1]
    d_v = p_a.shape[1]
    n_heads = d_v // v_dim
    group_dim = d_f // N_FOURIER_GROUPS
    tn = d_f
    assert w_in.shape[2] == 4 * tn + 2 * d and d_v == tn and n_heads * 2 * head_dim == tn
    assert 2 * head_dim == V7X_LANES and v_dim == V7X_LANES

    m = batch * seq
    tm_ffn = min(1024, seq)
    tm_in = min(512, seq)
    tm_merge = min(256, seq)
    t_dft = min(1024, seq)
    tq = min(512, seq)
    tk = min(2048, seq)
    tf = 512

    cmat, smat, wc = _dft_tables(seq, group_dim)
    rc, rs1, rs2 = _rope_lane_tables(seq, head_dim)
    blk = jnp.arange(V7X_MXU_DIM, dtype=jnp.int32) // head_dim
    e = (blk[:, None] == blk[None, :]).astype(BF16)
    q_scale = head_dim ** -0.5 * LOG2E

    bf = lambda w: w.astype(BF16)
    xs = x.reshape(m, d)
    for i in range(depth):
        lam_init = 0.8 - 0.6 * math.exp(-0.3 * i)
        xs = _ffn(xs, norm_ffa[i][None], bf(ffa_gate[i]), bf(ffa_up[i]), bf(ffa_down[i]),
                  norm_ffa[i][None], final_norm=False, tm=tm_ffn, tf=tf)
        qg = jnp.tile(q_norm[i], tn // head_dim)[None] * q_scale
        kg = jnp.tile(k_norm[i], tn // head_dim)[None]
        ab, qT, k, vT, sg = _inproj(xs, norm_mix[i][None], bf(w_in[i]), wc, e, qg, kg, rc, rs1, rs2,
                                    batch=batch, seq=seq, tm=tm_in, tn=tn, group_dim=group_dim,
                                    head_dim=head_dim)
        f = _fourier(cmat, smat, ab.reshape(batch, seq, 2 * tn), tm=t_dft, tk=t_dft)
        score_bound = head_dim * q_scale * jnp.max(jnp.abs(q_norm[i])) * jnp.max(jnp.abs(k_norm[i]))
        bounded = (score_bound <= SCORE_BOUND_LIMIT).astype(jnp.int32).reshape(1)
        o = _attention(bounded, qT, k.reshape(batch, seq, tn), vT, lambda_q1[i][None],
                       lambda_k1[i][None],
                       lambda_q2[i][None], lambda_k2[i][None], subln[i][:, None],
                       lam_init=lam_init, n_heads=n_heads, tq=tq, tk=tk)
        xs = _merge(xs, f.reshape(m, d_f), o.reshape(m, d_v), sg, bf(p_f[i]), bf(p_a[i]), bf(w_o[i]),
                    tm=tm_merge)
        xs = _ffn(xs, norm_ffb[i][None], bf(ffb_gate[i]), bf(ffb_up[i]), bf(ffb_down[i]),
                  norm_out[i][None], final_norm=True, tm=tm_ffn, tf=tf)
    return xs.reshape(batch, seq, d)
```

```python
import functools
import math

import jax
import jax.numpy as jnp
from jax import lax
from jax.experimental import pallas as pl
from jax.experimental.pallas import tpu as pltpu

F32 = jnp.float32
BF16 = jnp.bfloat16

N_FOURIER_GROUPS = 4
ROPE_THETA = 500000.0
EPS = 1e-6
LOG2E = 1.4426950408889634
NEG_BIG = -1e30
SCORE_BOUND_LIMIT = 64.0

V7X_LANES = 128
V7X_MXU_DIM = 256
V7X_VMEM_LIMIT_BYTES = 56 * 1024 * 1024


def _params(*semantics):
    return pltpu.CompilerParams(dimension_semantics=semantics,
                                vmem_limit_bytes=V7X_VMEM_LIMIT_BYTES)


def _rms_scale(x):
    return lax.rsqrt(jnp.mean(x * x, axis=-1, keepdims=True) + EPS)


def _ffn_kernel(x_ref, gin_ref, wg_ref, wu_ref, wd_ref, gout_ref, o_ref, h_ref, *, final_norm):
    j = pl.program_id(1)

    @pl.when(j == 0)
    def _():
        x = x_ref[...]
        h_ref[...] = (x * _rms_scale(x) * gin_ref[...]).astype(BF16)
        o_ref[...] = x

    h = h_ref[...]
    g = jnp.dot(h, wg_ref[...], preferred_element_type=F32)
    u = jnp.dot(h, wu_ref[...], preferred_element_type=F32)
    a = (0.5 * g) * jax.nn.sigmoid(g) * u
    o_ref[...] += jnp.dot(a.astype(BF16), wd_ref[...], preferred_element_type=F32)

    if final_norm:
        @pl.when(j == pl.num_programs(1) - 1)
        def _():
            y = o_ref[...]
            o_ref[...] = y * _rms_scale(y) * gout_ref[...]


def _ffn(x, g_in, wg, wu, wd, g_out, *, layer, final_norm, tm, tf):
    m, d = x.shape
    f = wg.shape[2]
    return pl.pallas_call(
        functools.partial(_ffn_kernel, final_norm=final_norm),
        grid=(m // tm, f // tf),
        in_specs=[
            pl.BlockSpec((tm, d), lambda i, j: (i, 0)),
            pl.BlockSpec((1, d), lambda i, j: (0, 0)),
            pl.BlockSpec((None, d, tf), lambda i, j: (layer, 0, j)),
            pl.BlockSpec((None, d, tf), lambda i, j: (layer, 0, j)),
            pl.BlockSpec((None, tf, d), lambda i, j: (layer, j, 0)),
            pl.BlockSpec((1, d), lambda i, j: (0, 0)),
        ],
        out_specs=pl.BlockSpec((tm, d), lambda i, j: (i, 0)),
        out_shape=jax.ShapeDtypeStruct((m, d), F32),
        scratch_shapes=[pltpu.VMEM((tm, d), BF16)],
        compiler_params=_params("parallel", "arbitrary"),
        name="ffn",
    )(x, g_in, wg, wu, wd, g_out)


def _norm_rope(z, e_ref, gain, rc, rs1, rs2, head_dim):
    tn = z.shape[1]
    zz = (z * z).astype(BF16)
    half = head_dim // 8
    outs = []
    for c in range(tn // V7X_MXU_DIM):
        cols = slice(c * V7X_MXU_DIM, (c + 1) * V7X_MXU_DIM)
        ss = jnp.dot(zz[:, cols], e_ref[...], preferred_element_type=F32)
        y = z[:, cols] * lax.rsqrt(ss * (1.0 / head_dim) + EPS) * gain[:, cols]
        for hh in range(V7X_MXU_DIM // V7X_LANES):
            yc = y[:, hh * V7X_LANES:(hh + 1) * V7X_LANES]
            outs.append(yc * rc + pltpu.roll(yc, V7X_LANES - half, 1) * rs1
                        + pltpu.roll(yc, half, 1) * rs2)
    return jnp.concatenate(outs, axis=1)


def _inproj_kernel(x_ref, gn_ref, w_ref, wc_ref, e_ref, qg_ref, kg_ref, rc_ref, rs1_ref, rs2_ref,
                   ab_ref, qT_ref, k_ref, vT_ref, sg_ref, h_ref, *, group_dim, head_dim):
    j = pl.program_id(1)

    @pl.when(j == 0)
    def _():
        x = x_ref[...]
        h_ref[...] = (x * _rms_scale(x) * gn_ref[...]).astype(BF16)

    z = jnp.dot(h_ref[...], w_ref[...], preferred_element_type=F32)

    @pl.when(j == 0)
    def _():
        u = z.astype(BF16)
        n_groups = z.shape[1] // group_dim
        for g in range(n_groups):
            t = jnp.dot(u[:, g * group_dim:(g + 1) * group_dim], wc_ref[...],
                        preferred_element_type=F32)
            ab_ref[:, g * group_dim:(g + 1) * group_dim] = t[:, :group_dim].astype(BF16)
            ab_ref[:, (n_groups + g) * group_dim:(n_groups + g + 1) * group_dim] = (
                t[:, group_dim:].astype(BF16))

    @pl.when(j == 1)
    def _():
        y = _norm_rope(z, e_ref, qg_ref[...], rc_ref[...], rs1_ref[...], rs2_ref[...], head_dim)
        qT_ref[0] = y.T.astype(BF16)

    @pl.when(j == 2)
    def _():
        y = _norm_rope(z, e_ref, kg_ref[...], rc_ref[...], rs1_ref[...], rs2_ref[...], head_dim)
        k_ref[...] = y.astype(BF16)

    @pl.when(j == 3)
    def _():
        vT_ref[0] = z.T.astype(BF16)

    @pl.when(j >= 4)
    def _():
        sg_ref[...] = jax.nn.sigmoid(z).astype(BF16)


def _inproj(x, g_norm, w_in, wc, e, qg, kg, rc, rs1, rs2, *, layer, batch, seq, tm, tn, group_dim,
            head_dim):
    m, d = x.shape
    n_blocks = w_in.shape[2] // tn
    n_gate_blocks = n_blocks - 4
    n_s = seq // tm
    const = lambda i, j: (0, 0)
    return pl.pallas_call(
        functools.partial(_inproj_kernel, group_dim=group_dim, head_dim=head_dim),
        grid=(m // tm, n_blocks),
        in_specs=[
            pl.BlockSpec((tm, d), lambda i, j: (i, 0)),
            pl.BlockSpec((1, d), const),
            pl.BlockSpec((None, d, tn), lambda i, j: (layer, 0, j)),
            pl.BlockSpec(wc.shape, const),
            pl.BlockSpec(e.shape, const),
            pl.BlockSpec((1, tn), const),
            pl.BlockSpec((1, tn), const),
            pl.BlockSpec((tm, V7X_LANES), lambda i, j: (i % n_s, 0)),
            pl.BlockSpec((tm, V7X_LANES), lambda i, j: (i % n_s, 0)),
            pl.BlockSpec((tm, V7X_LANES), lambda i, j: (i % n_s, 0)),
        ],
        out_specs=[
            pl.BlockSpec((tm, 2 * tn), lambda i, j: (i, 0)),
            pl.BlockSpec((1, tn, tm), lambda i, j: (i // n_s, 0, i % n_s)),
            pl.BlockSpec((tm, tn), lambda i, j: (i, 0)),
            pl.BlockSpec((1, tn, tm), lambda i, j: (i // n_s, 0, i % n_s)),
            pl.BlockSpec((tm, tn), lambda i, j: (i, jnp.maximum(j - 4, 0))),
        ],
        out_shape=[
            jax.ShapeDtypeStruct((m, 2 * tn), BF16),
            jax.ShapeDtypeStruct((batch, tn, seq), BF16),
            jax.ShapeDtypeStruct((m, tn), BF16),
            jax.ShapeDtypeStruct((batch, tn, seq), BF16),
            jax.ShapeDtypeStruct((m, n_gate_blocks * tn), BF16),
        ],
        scratch_shapes=[pltpu.VMEM((tm, d), BF16)],
        compiler_params=_params("parallel", "arbitrary"),
        name="inproj",
    )(x, g_norm, w_in, wc, e, qg, kg, rc, rs1, rs2)


def _fold_kernel(ab_ref, rev_ref, mid_ref, o_ref):
    w = mid_ref.shape[2]
    ab, rev = ab_ref[...], rev_ref[...]
    o_ref[:, :, :w] = ab[:, :, :w] + rev[:, :, :w]
    odd = ab[:, :, w:] - rev[:, :, w:]
    first = lax.broadcasted_iota(jnp.int32, odd.shape, 1) < jnp.where(pl.program_id(0) == 0, 1, 0)
    o_ref[:, :, w:] = jnp.where(first, jnp.broadcast_to(mid_ref[...], odd.shape), odd)


def _fold(ab, ab_rev, mid, *, t):
    batch, seq, two_w = ab.shape
    blk = pl.BlockSpec((batch, t, two_w), lambda i: (0, i, 0))
    return pl.pallas_call(
        _fold_kernel,
        grid=(seq // 2 // t,),
        in_specs=[blk, blk, pl.BlockSpec(mid.shape, lambda i: (0, 0, 0))],
        out_specs=blk,
        out_shape=jax.ShapeDtypeStruct((batch, seq // 2, two_w), BF16),
        compiler_params=_params("parallel"),
        name="fold",
    )(ab, ab_rev, mid)


def _fourier_kernel(c_ref, s_ref, a_ref, b_ref, o_ref, acc_ref):
    kk = pl.program_id(1)

    @pl.when(kk == 0)
    def _():
        acc_ref[...] = jnp.zeros_like(acc_ref)

    for b in range(a_ref.shape[0]):
        acc_ref[b] += (jnp.dot(c_ref[...], a_ref[b], preferred_element_type=F32)
                       + jnp.dot(s_ref[...], b_ref[b], preferred_element_type=F32))

    @pl.when(kk == pl.num_programs(1) - 1)
    def _():
        o_ref[...] = acc_ref[...].astype(o_ref.dtype)


def _fourier(cmat, smat, folded, *, tm, tk):
    batch, half_seq, two_w = folded.shape
    seq = cmat.shape[0]
    w = two_w // 2
    return pl.pallas_call(
        _fourier_kernel,
        grid=(seq // tm, half_seq // tk),
        in_specs=[
            pl.BlockSpec((tm, tk), lambda i, kk: (i, kk)),
            pl.BlockSpec((tm, tk), lambda i, kk: (i, kk)),
            pl.BlockSpec((batch, tk, w), lambda i, kk: (0, kk, 0)),
            pl.BlockSpec((batch, tk, w), lambda i, kk: (0, kk, 1)),
        ],
        out_specs=pl.BlockSpec((batch, tm, w), lambda i, kk: (0, i, 0)),
        out_shape=jax.ShapeDtypeStruct((batch, seq, w), BF16),
        scratch_shapes=[pltpu.VMEM((batch, tm, w), F32)],
        compiler_params=_params("parallel", "arbitrary"),
        name="fourier",
    )(cmat, smat, folded, folded)


def _attn_kernel(bounded_ref, qT_ref, k_ref, vT_ref, lq1_ref, lk1_ref, lq2_ref, lk2_ref, sub_ref,
                 o_ref, m_ref, l_ref, acc_ref, *, tk, lam_init):
    qT = qT_ref[0]
    qk_dim = qT.shape[0] // 2
    tq = qT.shape[1]
    row = lax.broadcasted_iota(jnp.int32, qT.shape, 0)
    zero = jnp.zeros_like(qT)
    q2 = jnp.concatenate([jnp.where(row < qk_dim, qT, zero), jnp.where(row >= qk_dim, qT, zero)],
                         axis=1)
    n_kb = k_ref.shape[1] // tk

    l_ref[...] = jnp.zeros_like(l_ref)
    acc_ref[...] = jnp.zeros_like(acc_ref)

    def blocks(j):
        start = pl.multiple_of(j * tk, tk)
        return k_ref[0, pl.ds(start, tk), :], vT_ref[0, :, pl.ds(start, tk)]

    @pl.when(bounded_ref[0] == 1)
    def _():
        def body(j, carry):
            kb, vb = blocks(j)
            s = jnp.dot(kb, q2, preferred_element_type=F32)
            p = jnp.exp2(s)
            l_ref[...] += jnp.sum(p.reshape(tk // 8, 8, 2 * tq), axis=0)
            acc_ref[...] += jnp.dot(vb, p.astype(BF16), preferred_element_type=F32)
            return carry

        lax.fori_loop(0, n_kb, body, 0)

    @pl.when(bounded_ref[0] != 1)
    def _():
        m_ref[...] = jnp.full_like(m_ref, NEG_BIG)

        def body(j, carry):
            kb, vb = blocks(j)
            s = jnp.dot(kb, q2, preferred_element_type=F32)
            m_prev = m_ref[...]
            m_new = jnp.maximum(m_prev, jnp.max(s, axis=0, keepdims=True))
            alpha = jnp.exp2(m_prev - m_new)
            p = jnp.exp2(s - m_new)
            l_ref[...] = alpha * l_ref[...] + jnp.sum(p.reshape(tk // 8, 8, 2 * tq), axis=0)
            acc_ref[...] = acc_ref[...] * alpha + jnp.dot(vb, p.astype(BF16),
                                                          preferred_element_type=F32)
            m_ref[...] = m_new
            return carry

        lax.fori_loop(0, n_kb, body, 0)

    lam = (jnp.exp(jnp.sum(lq1_ref[...] * lk1_ref[...], axis=-1, keepdims=True))
           - jnp.exp(jnp.sum(lq2_ref[...] * lk2_ref[...], axis=-1, keepdims=True)) + lam_init)
    l = jnp.sum(l_ref[...], axis=0, keepdims=True)
    o = acc_ref[:, :tq] * (1.0 / l[:, :tq]) - acc_ref[:, tq:] * (lam / l[:, tq:])
    scale = lax.rsqrt(jnp.mean(o * o, axis=0, keepdims=True) + EPS)
    y = o * scale * (sub_ref[...] * (1.0 - lam_init))
    o_ref[0] = y.T.astype(o_ref.dtype)


def _attention(bounded, qT, k, vT, lq1, lk1, lq2, lk2, sub, *, lam_init, n_heads, tq, tk):
    batch, d_qk, seq = qT.shape
    d_v = vT.shape[1]
    hq, hv = d_qk // n_heads, d_v // n_heads
    vec = lambda b, h, i, flag: (0, 0)
    return pl.pallas_call(
        functools.partial(_attn_kernel, tk=tk, lam_init=lam_init),
        grid_spec=pltpu.PrefetchScalarGridSpec(
            num_scalar_prefetch=1,
            grid=(batch, n_heads, seq // tq),
            in_specs=[
                pl.BlockSpec((1, hq, tq), lambda b, h, i, flag: (b, h, i)),
                pl.BlockSpec((1, seq, hq), lambda b, h, i, flag: (b, 0, h)),
                pl.BlockSpec((1, hv, seq), lambda b, h, i, flag: (b, h, 0)),
                pl.BlockSpec(lq1.shape, vec),
                pl.BlockSpec(lk1.shape, vec),
                pl.BlockSpec(lq2.shape, vec),
                pl.BlockSpec(lk2.shape, vec),
                pl.BlockSpec(sub.shape, vec),
            ],
            out_specs=pl.BlockSpec((1, tq, hv), lambda b, h, i, flag: (b, i, h)),
            scratch_shapes=[
                pltpu.VMEM((1, 2 * tq), F32),
                pltpu.VMEM((8, 2 * tq), F32),
                pltpu.VMEM((hv, 2 * tq), F32),
            ],
        ),
        out_shape=jax.ShapeDtypeStruct((batch, seq, d_v), BF16),
        compiler_params=_params("parallel", "parallel", "arbitrary"),
        name="attn",
    )(bounded, qT, k, vT, lq1, lk1, lq2, lk2, sub)


def _merge_kernel(x_ref, f_ref, o_ref, sg_ref, pf_ref, pa_ref, wo_ref, out_ref):
    d = x_ref.shape[1]
    bf = jnp.dot(f_ref[...], pf_ref[...], preferred_element_type=F32)
    ba = jnp.dot(o_ref[...], pa_ref[...], preferred_element_type=F32)
    mix = sg_ref[:, :d].astype(F32) * bf + sg_ref[:, d:].astype(F32) * ba
    out_ref[...] = x_ref[...] + jnp.dot(mix.astype(BF16), wo_ref[...], preferred_element_type=F32)


def _merge(x, f, o, sg, p_f, p_a, w_o, *, layer, tm):
    m, d = x.shape
    row = lambda i: (i, 0)
    stacked = lambda w: pl.BlockSpec((None,) + w.shape[1:], lambda i: (layer, 0, 0))
    return pl.pallas_call(
        _merge_kernel,
        grid=(m // tm,),
        in_specs=[
            pl.BlockSpec((tm, d), row),
            pl.BlockSpec((tm, f.shape[1]), row),
            pl.BlockSpec((tm, o.shape[1]), row),
            pl.BlockSpec((tm, sg.shape[1]), row),
            stacked(p_f),
            stacked(p_a),
            stacked(w_o),
        ],
        out_specs=pl.BlockSpec((tm, d), row),
        out_shape=jax.ShapeDtypeStruct((m, d), F32),
        compiler_params=_params("parallel"),
        name="merge",
    )(x, f, o, sg, p_f, p_a, w_o)


def _dft_tables(seq, group_dim):
    s = jnp.arange(seq // 2, dtype=jnp.int32)[None, :]
    hi = jnp.arange(seq // V7X_LANES, dtype=jnp.int32)[:, None]
    lo = jnp.arange(V7X_LANES, dtype=jnp.int32)[:, None]
    ang_hi = ((hi * V7X_LANES * s) % seq).astype(F32) * (2.0 * math.pi / seq)
    ang_lo = ((lo * s) % seq).astype(F32) * (2.0 * math.pi / seq)
    ca, sa = jnp.cos(ang_hi)[:, None, :], jnp.sin(ang_hi)[:, None, :]
    cb, sb = jnp.cos(ang_lo)[None, :, :], jnp.sin(ang_lo)[None, :, :]
    k_par = (1 - 2 * (jnp.arange(seq, dtype=jnp.int32) % 2)).astype(F32)[:, None]
    col0 = s == 0
    cmat = jnp.where(col0, 0.5, (ca * cb - sa * sb).reshape(seq, seq // 2)).astype(BF16)
    smat = jnp.where(col0, k_par, (-(sa * cb + ca * sb)).reshape(seq, seq // 2)).astype(BF16)
    c = jnp.arange(group_dim, dtype=jnp.int32)
    angc = ((c[:, None] * c[None, :]) % group_dim).astype(F32) * (2.0 * math.pi / group_dim)
    ortho = 1.0 / math.sqrt(seq * group_dim)
    wc = (jnp.concatenate([jnp.cos(angc), jnp.sin(angc)], axis=1) * ortho).astype(BF16)
    return cmat, smat, wc


def _rope_lane_tables(seq, head_dim):
    rope_dim = head_dim // 4
    half = rope_dim // 2
    pos = jnp.arange(seq, dtype=F32)
    inv_freq = ROPE_THETA ** (-jnp.arange(0, rope_dim, 2, dtype=F32) / rope_dim)
    ang = pos[:, None] * inv_freq[None, :]
    cos, sin = jnp.cos(ang), jnp.sin(ang)
    ones = jnp.ones((seq, head_dim - rope_dim), F32)
    zeros_h = jnp.zeros((seq, half), F32)
    zeros_r = jnp.zeros((seq, head_dim - rope_dim), F32)
    rc = jnp.concatenate([cos, cos, ones], axis=1)
    rs1 = jnp.concatenate([-sin, zeros_h, zeros_r], axis=1)
    rs2 = jnp.concatenate([zeros_h, sin, zeros_r], axis=1)
    reps = V7X_LANES // head_dim
    return tuple(jnp.tile(t, (1, reps)) for t in (rc, rs1, rs2))


def kernel(x, norm_ffa, ffa_gate, ffa_up, ffa_down, norm_mix, w_in, q_norm, k_norm, lambda_q1,
           lambda_k1, lambda_q2, lambda_k2, subln, p_f, p_a, w_o, norm_ffb, ffb_gate, ffb_up,
           ffb_down, norm_out):
    batch, seq, d = x.shape
    depth = w_in.shape[0]
    head_dim = q_norm.shape[1]
    v_dim = subln.shape[1]
    d_f = p_f.shape[1]
    d_v = p_a.shape[1]
    n_heads = d_v // v_dim
    group_dim = d_f // N_FOURIER_GROUPS
    tn = d_f
    assert w_in.shape[2] == 4 * tn + 2 * d and d_v == tn and n_heads * 2 * head_dim == tn
    assert 2 * head_dim == V7X_LANES and v_dim == V7X_LANES

    m = batch * seq
    tm_ffn = min(512, seq)
    tm_in = min(512, seq)
    tm_merge = min(256, seq)
    t_dft = min(1024, seq)
    t_fold = min(512, seq // 2)
    tq = min(512, seq)
    tk = min(2048, seq)
    tf = 512

    cmat, smat, wc = _dft_tables(seq, group_dim)
    rc, rs1, rs2 = _rope_lane_tables(seq, head_dim)
    blk = jnp.arange(V7X_MXU_DIM, dtype=jnp.int32) // head_dim
    e = (blk[:, None] == blk[None, :]).astype(BF16)
    q_scale = head_dim ** -0.5 * LOG2E

    ffa_gate, ffa_up, ffa_down, w_in, p_f, p_a, w_o, ffb_gate, ffb_up, ffb_down = (
        w.astype(BF16) for w in (ffa_gate, ffa_up, ffa_down, w_in, p_f, p_a, w_o, ffb_gate, ffb_up,
                                 ffb_down))
    xs = x.reshape(m, d)
    for i in range(depth):
        lam_init = 0.8 - 0.6 * math.exp(-0.3 * i)
        xs = _ffn(xs, norm_ffa[i][None], ffa_gate, ffa_up, ffa_down, norm_ffa[i][None], layer=i,
                  final_norm=False, tm=tm_ffn, tf=tf)
        qg = jnp.tile(q_norm[i], tn // head_dim)[None] * q_scale
        kg = jnp.tile(k_norm[i], tn // head_dim)[None]
        ab, qT, k, vT, sg = _inproj(xs, norm_mix[i][None], w_in, wc, e, qg, kg, rc, rs1, rs2,
                                    layer=i, batch=batch, seq=seq, tm=tm_in, tn=tn,
                                    group_dim=group_dim, head_dim=head_dim)
        ab = ab.reshape(batch, seq, 2 * tn)
        ab_rev = jnp.roll(jnp.flip(ab, axis=1), 1, axis=1)
        folded = _fold(ab, ab_rev, ab[:, seq // 2:seq // 2 + 1, :tn], t=t_fold)
        f = _fourier(cmat, smat, folded, tm=t_dft, tk=min(t_dft, seq // 2))
        score_bound = head_dim * q_scale * jnp.max(jnp.abs(q_norm[i])) * jnp.max(jnp.abs(k_norm[i]))
        bounded = (score_bound <= SCORE_BOUND_LIMIT).astype(jnp.int32).reshape(1)
        o = _attention(bounded, qT, k.reshape(batch, seq, tn), vT, lambda_q1[i][None],
                       lambda_k1[i][None], lambda_q2[i][None], lambda_k2[i][None],
                       subln[i][:, None], lam_init=lam_init, n_heads=n_heads, tq=tq, tk=tk)
        xs = _merge(xs, f.reshape(m, d_f), o.reshape(m, d_v), sg, p_f, p_a, w_o, layer=i,
                    tm=tm_merge)
        xs = _ffn(xs, norm_ffb[i][None], ffb_gate, ffb_up, ffb_down, norm_out[i][None], layer=i,
                  final_norm=True, tm=tm_ffn, tf=tf)
    return xs.reshape(batch, seq, d)
```

```python
import functools
import math

import jax
import jax.numpy as jnp
from jax import lax
from jax.experimental import pallas as pl
from jax.experimental.pallas import tpu as pltpu

F32 = jnp.float32
BF16 = jnp.bfloat16

N_FOURIER_GROUPS = 4
ROPE_THETA = 500000.0
EPS = 1e-6
LOG2E = 1.4426950408889634
NEG_BIG = -1e30
SCORE_BOUND_LIMIT = 64.0

V7X_LANES = 128
V7X_MXU_DIM = 256
V7X_VMEM_LIMIT_BYTES = 56 * 1024 * 1024


def _params(*semantics):
    return pltpu.CompilerParams(dimension_semantics=semantics,
                                vmem_limit_bytes=V7X_VMEM_LIMIT_BYTES)


def _rms_scale(x):
    return lax.rsqrt(jnp.mean(x * x, axis=-1, keepdims=True) + EPS)


def _ffn_kernel(x_ref, gin_ref, wg_ref, wu_ref, wd_ref, gout_ref, o_ref, h_ref, *, final_norm):
    j = pl.program_id(1)

    @pl.when(j == 0)
    def _():
        x = x_ref[...]
        h_ref[...] = (x * _rms_scale(x) * gin_ref[...]).astype(BF16)
        o_ref[...] = x

    h = h_ref[...]
    g = jnp.dot(h, wg_ref[...], preferred_element_type=F32)
    u = jnp.dot(h, wu_ref[...], preferred_element_type=F32)
    a = (0.5 * g) * jax.nn.sigmoid(g) * u
    o_ref[...] += jnp.dot(a.astype(BF16), wd_ref[...], preferred_element_type=F32)

    if final_norm:
        @pl.when(j == pl.num_programs(1) - 1)
        def _():
            y = o_ref[...]
            o_ref[...] = y * _rms_scale(y) * gout_ref[...]


def _ffn(x, g_in, wg, wu, wd, g_out, *, layer, final_norm, tm, tf):
    m, d = x.shape
    f = wg.shape[2]
    return pl.pallas_call(
        functools.partial(_ffn_kernel, final_norm=final_norm),
        grid=(m // tm, f // tf),
        in_specs=[
            pl.BlockSpec((tm, d), lambda i, j: (i, 0)),
            pl.BlockSpec((1, d), lambda i, j: (0, 0)),
            pl.BlockSpec((None, d, tf), lambda i, j: (layer, 0, j)),
            pl.BlockSpec((None, d, tf), lambda i, j: (layer, 0, j)),
            pl.BlockSpec((None, tf, d), lambda i, j: (layer, j, 0)),
            pl.BlockSpec((1, d), lambda i, j: (0, 0)),
        ],
        out_specs=pl.BlockSpec((tm, d), lambda i, j: (i, 0)),
        out_shape=jax.ShapeDtypeStruct((m, d), F32),
        scratch_shapes=[pltpu.VMEM((tm, d), BF16)],
        compiler_params=_params("parallel", "arbitrary"),
        name="ffn",
    )(x, g_in, wg, wu, wd, g_out)


def _norm_rope(z, e_ref, gain, rc, rs1, rs2, head_dim):
    tn = z.shape[1]
    zz = (z * z).astype(BF16)
    half = head_dim // 8
    outs = []
    for c in range(tn // V7X_MXU_DIM):
        cols = slice(c * V7X_MXU_DIM, (c + 1) * V7X_MXU_DIM)
        ss = jnp.dot(zz[:, cols], e_ref[...], preferred_element_type=F32)
        y = z[:, cols] * lax.rsqrt(ss * (1.0 / head_dim) + EPS) * gain[:, cols]
        for hh in range(V7X_MXU_DIM // V7X_LANES):
            yc = y[:, hh * V7X_LANES:(hh + 1) * V7X_LANES]
            outs.append(yc * rc + pltpu.roll(yc, V7X_LANES - half, 1) * rs1
                        + pltpu.roll(yc, half, 1) * rs2)
    return jnp.concatenate(outs, axis=1)


def _inproj_kernel(x_ref, gn_ref, w_ref, wc_ref, e_ref, qg_ref, kg_ref, rc_ref, rs1_ref, rs2_ref,
                   ab_ref, qT_ref, k_ref, vT_ref, sg_ref, h_ref, z_ref, *, group_dim, head_dim,
                   n_blocks):
    j = pl.program_id(1)

    def epilogue(kind, z):
        if kind == "fourier":
            u = z.astype(BF16)
            n_groups = z.shape[1] // group_dim
            for g in range(n_groups):
                t = jnp.dot(u[:, g * group_dim:(g + 1) * group_dim], wc_ref[...],
                            preferred_element_type=F32)
                ab_ref[:, g * group_dim:(g + 1) * group_dim] = t[:, :group_dim].astype(BF16)
                ab_ref[:, (n_groups + g) * group_dim:(n_groups + g + 1) * group_dim] = (
                    t[:, group_dim:].astype(BF16))
        elif kind == "q":
            y = _norm_rope(z, e_ref, qg_ref[...], rc_ref[...], rs1_ref[...], rs2_ref[...], head_dim)
            qT_ref[0] = y.T.astype(BF16)
        elif kind == "k":
            y = _norm_rope(z, e_ref, kg_ref[...], rc_ref[...], rs1_ref[...], rs2_ref[...], head_dim)
            k_ref[...] = y.astype(BF16)
        elif kind == "v":
            vT_ref[0] = z.T.astype(BF16)
        else:
            sg_ref[...] = jax.nn.sigmoid(z).astype(BF16)

    kinds = ("fourier", "q", "k", "v") + ("gate",) * (n_blocks - 4)
    for step in range(n_blocks + 1):
        @pl.when(j == step)
        def _(step=step):
            if step == 0:
                x = x_ref[...]
                h_ref[...] = (x * _rms_scale(x) * gn_ref[...]).astype(BF16)
            if step > 0:
                epilogue(kinds[step - 1], z_ref[(step - 1) % 2])
            if step < n_blocks:
                z_ref[step % 2] = jnp.dot(h_ref[...], w_ref[...], preferred_element_type=F32)


def _inproj(x, g_norm, w_in, wc, e, qg, kg, rc, rs1, rs2, *, layer, batch, seq, tm, tn, group_dim,
            head_dim):
    m, d = x.shape
    n_blocks = w_in.shape[2] // tn
    n_gate_blocks = n_blocks - 4
    n_s = seq // tm
    const = lambda i, j: (0, 0)
    return pl.pallas_call(
        functools.partial(_inproj_kernel, group_dim=group_dim, head_dim=head_dim,
                          n_blocks=n_blocks),
        grid=(m // tm, n_blocks + 1),
        in_specs=[
            pl.BlockSpec((tm, d), lambda i, j: (i, 0)),
            pl.BlockSpec((1, d), const),
            pl.BlockSpec((None, d, tn), lambda i, j: (layer, 0, jnp.minimum(j, n_blocks - 1))),
            pl.BlockSpec(wc.shape, const),
            pl.BlockSpec(e.shape, const),
            pl.BlockSpec((1, tn), const),
            pl.BlockSpec((1, tn), const),
            pl.BlockSpec((tm, V7X_LANES), lambda i, j: (i % n_s, 0)),
            pl.BlockSpec((tm, V7X_LANES), lambda i, j: (i % n_s, 0)),
            pl.BlockSpec((tm, V7X_LANES), lambda i, j: (i % n_s, 0)),
        ],
        out_specs=[
            pl.BlockSpec((tm, 2 * tn), lambda i, j: (i, 0)),
            pl.BlockSpec((1, tn, tm), lambda i, j: (i // n_s, 0, i % n_s)),
            pl.BlockSpec((tm, tn), lambda i, j: (i, 0)),
            pl.BlockSpec((1, tn, tm), lambda i, j: (i // n_s, 0, i % n_s)),
            pl.BlockSpec((tm, tn), lambda i, j: (i, jnp.maximum(j - 5, 0))),
        ],
        out_shape=[
            jax.ShapeDtypeStruct((m, 2 * tn), BF16),
            jax.ShapeDtypeStruct((batch, tn, seq), BF16),
            jax.ShapeDtypeStruct((m, tn), BF16),
            jax.ShapeDtypeStruct((batch, tn, seq), BF16),
            jax.ShapeDtypeStruct((m, n_gate_blocks * tn), BF16),
        ],
        scratch_shapes=[pltpu.VMEM((tm, d), BF16), pltpu.VMEM((2, tm, tn), F32)],
        compiler_params=_params("parallel", "arbitrary"),
        name="inproj",
    )(x, g_norm, w_in, wc, e, qg, kg, rc, rs1, rs2)


def _fold_kernel(anti_ref, pick_ref, lo_ref, hi_ref, above_ref, mid_ref, o_ref):
    w = lo_ref.shape[2] // 2
    first = jnp.where(pl.program_id(0) == 0, 1.0, 0.0)
    for b in range(lo_ref.shape[0]):
        rev = (jnp.dot(anti_ref[...], hi_ref[b], preferred_element_type=F32)
               + jnp.dot(pick_ref[...], above_ref[b], preferred_element_type=F32))
        mid = jnp.dot(pick_ref[...], mid_ref[b], preferred_element_type=F32)
        lo = lo_ref[b].astype(F32)
        o_ref[b, :, :w] = (lo[:, :w] + rev[:, :w]).astype(BF16)
        o_ref[b, :, w:] = (lo[:, w:] - rev[:, w:] + first * mid[:, :w]).astype(BF16)


def _fold(ab, *, t):
    batch, seq, two_w = ab.shape
    nb = seq // t
    sub = 16
    r = jnp.arange(t, dtype=jnp.int32)
    anti = (r[:, None] + r[None, :] == t).astype(BF16)
    pick = (r[:, None] + jnp.arange(sub, dtype=jnp.int32)[None, :] == 0).astype(BF16)
    return pl.pallas_call(
        _fold_kernel,
        grid=(nb // 2,),
        in_specs=[
            pl.BlockSpec((t, t), lambda i: (0, 0)),
            pl.BlockSpec((t, sub), lambda i: (0, 0)),
            pl.BlockSpec((batch, t, two_w), lambda i: (0, i, 0)),
            pl.BlockSpec((batch, t, two_w), lambda i: (0, nb - 1 - i, 0)),
            pl.BlockSpec((batch, sub, two_w), lambda i: (0, ((nb - i) * (t // sub)) % (seq // sub), 0)),
            pl.BlockSpec((batch, sub, two_w), lambda i: (0, seq // 2 // sub, 0)),
        ],
        out_specs=pl.BlockSpec((batch, t, two_w), lambda i: (0, i, 0)),
        out_shape=jax.ShapeDtypeStruct((batch, seq // 2, two_w), BF16),
        compiler_params=_params("parallel"),
        name="fold",
    )(anti, pick, ab, ab, ab, ab)


def _fourier_kernel(c_ref, s_ref, a_ref, b_ref, o_ref, acc_ref):
    kk = pl.program_id(1)

    @pl.when(kk == 0)
    def _():
        acc_ref[...] = jnp.zeros_like(acc_ref)

    for b in range(a_ref.shape[0]):
        acc_ref[b] += (jnp.dot(c_ref[...], a_ref[b], preferred_element_type=F32)
                       + jnp.dot(s_ref[...], b_ref[b], preferred_element_type=F32))

    @pl.when(kk == pl.num_programs(1) - 1)
    def _():
        o_ref[...] = acc_ref[...].astype(o_ref.dtype)


def _fourier(cmat, smat, folded, *, tm, tk):
    batch, half_seq, two_w = folded.shape
    seq = cmat.shape[0]
    w = two_w // 2
    return pl.pallas_call(
        _fourier_kernel,
        grid=(seq // tm, half_seq // tk),
        in_specs=[
            pl.BlockSpec((tm, tk), lambda i, kk: (i, kk)),
            pl.BlockSpec((tm, tk), lambda i, kk: (i, kk)),
            pl.BlockSpec((batch, tk, w), lambda i, kk: (0, kk, 0)),
            pl.BlockSpec((batch, tk, w), lambda i, kk: (0, kk, 1)),
        ],
        out_specs=pl.BlockSpec((batch, tm, w), lambda i, kk: (0, i, 0)),
        out_shape=jax.ShapeDtypeStruct((batch, seq, w), BF16),
        scratch_shapes=[pltpu.VMEM((batch, tm, w), F32)],
        compiler_params=_params("parallel", "arbitrary"),
        name="fourier",
    )(cmat, smat, folded, folded)


def _attn_kernel(bounded_ref, qT_ref, k_ref, vT_ref, lq1_ref, lk1_ref, lq2_ref, lk2_ref, sub_ref,
                 o_ref, m_ref, l_ref, acc_ref, *, tk, lam_init):
    qT = qT_ref[0]
    qk_dim = qT.shape[0] // 2
    tq = qT.shape[1]
    row = lax.broadcasted_iota(jnp.int32, qT.shape, 0)
    zero = jnp.zeros_like(qT)
    q2 = jnp.concatenate([jnp.where(row < qk_dim, qT, zero), jnp.where(row >= qk_dim, qT, zero)],
                         axis=1)
    n_kb = k_ref.shape[1] // tk

    l_ref[...] = jnp.zeros_like(l_ref)
    acc_ref[...] = jnp.zeros_like(acc_ref)

    def blocks(j):
        start = pl.multiple_of(j * tk, tk)
        return k_ref[0, pl.ds(start, tk), :], vT_ref[0, :, pl.ds(start, tk)]

    @pl.when(bounded_ref[0] == 1)
    def _():
        def body(j, carry):
            kb, vb = blocks(j)
            s = jnp.dot(kb, q2, preferred_element_type=F32)
            p = jnp.exp2(s)
            l_ref[...] += jnp.sum(p.reshape(tk // 8, 8, 2 * tq), axis=0)
            acc_ref[...] += jnp.dot(vb, p.astype(BF16), preferred_element_type=F32)
            return carry

        lax.fori_loop(0, n_kb, body, 0)

    @pl.when(bounded_ref[0] != 1)
    def _():
        m_ref[...] = jnp.full_like(m_ref, NEG_BIG)

        def body(j, carry):
            kb, vb = blocks(j)
            s = jnp.dot(kb, q2, preferred_element_type=F32)
            m_prev = m_ref[...]
            m_new = jnp.maximum(m_prev, jnp.max(s, axis=0, keepdims=True))
            alpha = jnp.exp2(m_prev - m_new)
            p = jnp.exp2(s - m_new)
            l_ref[...] = alpha * l_ref[...] + jnp.sum(p.reshape(tk // 8, 8, 2 * tq), axis=0)
            acc_ref[...] = acc_ref[...] * alpha + jnp.dot(vb, p.astype(BF16),
                                                          preferred_element_type=F32)
            m_ref[...] = m_new
            return carry

        lax.fori_loop(0, n_kb, body, 0)

    lam = (jnp.exp(jnp.sum(lq1_ref[...] * lk1_ref[...], axis=-1, keepdims=True))
           - jnp.exp(jnp.sum(lq2_ref[...] * lk2_ref[...], axis=-1, keepdims=True)) + lam_init)
    l = jnp.sum(l_ref[...], axis=0, keepdims=True)
    o = acc_ref[:, :tq] * (1.0 / l[:, :tq]) - acc_ref[:, tq:] * (lam / l[:, tq:])
    scale = lax.rsqrt(jnp.mean(o * o, axis=0, keepdims=True) + EPS)
    y = o * scale * (sub_ref[...] * (1.0 - lam_init))
    o_ref[0] = y.T.astype(o_ref.dtype)


def _attention(bounded, qT, k, vT, lq1, lk1, lq2, lk2, sub, *, lam_init, n_heads, tq, tk):
    batch, d_qk, seq = qT.shape
    d_v = vT.shape[1]
    hq, hv = d_qk // n_heads, d_v // n_heads
    vec = lambda b, h, i, flag: (0, 0)
    return pl.pallas_call(
        functools.partial(_attn_kernel, tk=tk, lam_init=lam_init),
        grid_spec=pltpu.PrefetchScalarGridSpec(
            num_scalar_prefetch=1,
            grid=(batch, n_heads, seq // tq),
            in_specs=[
                pl.BlockSpec((1, hq, tq), lambda b, h, i, flag: (b, h, i)),
                pl.BlockSpec((1, seq, hq), lambda b, h, i, flag: (b, 0, h)),
                pl.BlockSpec((1, hv, seq), lambda b, h, i, flag: (b, h, 0)),
                pl.BlockSpec(lq1.shape, vec),
                pl.BlockSpec(lk1.shape, vec),
                pl.BlockSpec(lq2.shape, vec),
                pl.BlockSpec(lk2.shape, vec),
                pl.BlockSpec(sub.shape, vec),
            ],
            out_specs=pl.BlockSpec((1, tq, hv), lambda b, h, i, flag: (b, i, h)),
            scratch_shapes=[
                pltpu.VMEM((1, 2 * tq), F32),
                pltpu.VMEM((8, 2 * tq), F32),
                pltpu.VMEM((hv, 2 * tq), F32),
            ],
        ),
        out_shape=jax.ShapeDtypeStruct((batch, seq, d_v), BF16),
        compiler_params=_params("parallel", "parallel", "arbitrary"),
        name="attn",
    )(bounded, qT, k, vT, lq1, lk1, lq2, lk2, sub)


def _merge_kernel(x_ref, f_ref, o_ref, sg_ref, pf_ref, pa_ref, wo_ref, out_ref):
    d = x_ref.shape[1]
    bf = jnp.dot(f_ref[...], pf_ref[...], preferred_element_type=F32)
    ba = jnp.dot(o_ref[...], pa_ref[...], preferred_element_type=F32)
    mix = sg_ref[:, :d].astype(F32) * bf + sg_ref[:, d:].astype(F32) * ba
    out_ref[...] = x_ref[...] + jnp.dot(mix.astype(BF16), wo_ref[...], preferred_element_type=F32)


def _merge(x, f, o, sg, p_f, p_a, w_o, *, layer, tm):
    m, d = x.shape
    row = lambda i: (i, 0)
    stacked = lambda w: pl.BlockSpec((None,) + w.shape[1:], lambda i: (layer, 0, 0))
    return pl.pallas_call(
        _merge_kernel,
        grid=(m // tm,),
        in_specs=[
            pl.BlockSpec((tm, d), row),
            pl.BlockSpec((tm, f.shape[1]), row),
            pl.BlockSpec((tm, o.shape[1]), row),
            pl.BlockSpec((tm, sg.shape[1]), row),
            stacked(p_f),
            stacked(p_a),
            stacked(w_o),
        ],
        out_specs=pl.BlockSpec((tm, d), row),
        out_shape=jax.ShapeDtypeStruct((m, d), F32),
        compiler_params=_params("parallel"),
        name="merge",
    )(x, f, o, sg, p_f, p_a, w_o)


def _dft_tables(seq, group_dim):
    s = jnp.arange(seq // 2, dtype=jnp.int32)[None, :]
    hi = jnp.arange(seq // V7X_LANES, dtype=jnp.int32)[:, None]
    lo = jnp.arange(V7X_LANES, dtype=jnp.int32)[:, None]
    ang_hi = ((hi * V7X_LANES * s) % seq).astype(F32) * (2.0 * math.pi / seq)
    ang_lo = ((lo * s) % seq).astype(F32) * (2.0 * math.pi / seq)
    ca, sa = jnp.cos(ang_hi)[:, None, :], jnp.sin(ang_hi)[:, None, :]
    cb, sb = jnp.cos(ang_lo)[None, :, :], jnp.sin(ang_lo)[None, :, :]
    k_par = (1 - 2 * (jnp.arange(seq, dtype=jnp.int32) % 2)).astype(F32)[:, None]
    col0 = s == 0
    cmat = jnp.where(col0, 0.5, (ca * cb - sa * sb).reshape(seq, seq // 2)).astype(BF16)
    smat = jnp.where(col0, k_par, (-(sa * cb + ca * sb)).reshape(seq, seq // 2)).astype(BF16)
    c = jnp.arange(group_dim, dtype=jnp.int32)
    angc = ((c[:, None] * c[None, :]) % group_dim).astype(F32) * (2.0 * math.pi / group_dim)
    ortho = 1.0 / math.sqrt(seq * group_dim)
    wc = (jnp.concatenate([jnp.cos(angc), jnp.sin(angc)], axis=1) * ortho).astype(BF16)
    return cmat, smat, wc


def _rope_lane_tables(seq, head_dim):
    rope_dim = head_dim // 4
    half = rope_dim // 2
    pos = jnp.arange(seq, dtype=F32)
    inv_freq = ROPE_THETA ** (-jnp.arange(0, rope_dim, 2, dtype=F32) / rope_dim)
    ang = pos[:, None] * inv_freq[None, :]
    cos, sin = jnp.cos(ang), jnp.sin(ang)
    ones = jnp.ones((seq, head_dim - rope_dim), F32)
    zeros_h = jnp.zeros((seq, half), F32)
    zeros_r = jnp.zeros((seq, head_dim - rope_dim), F32)
    rc = jnp.concatenate([cos, cos, ones], axis=1)
    rs1 = jnp.concatenate([-sin, zeros_h, zeros_r], axis=1)
    rs2 = jnp.concatenate([zeros_h, sin, zeros_r], axis=1)
    reps = V7X_LANES // head_dim
    return tuple(jnp.tile(t, (1, reps)) for t in (rc, rs1, rs2))


def kernel(x, norm_ffa, ffa_gate, ffa_up, ffa_down, norm_mix, w_in, q_norm, k_norm, lambda_q1,
           lambda_k1, lambda_q2, lambda_k2, subln, p_f, p_a, w_o, norm_ffb, ffb_gate, ffb_up,
           ffb_down, norm_out):
    batch, seq, d = x.shape
    depth = w_in.shape[0]
    head_dim = q_norm.shape[1]
    v_dim = subln.shape[1]
    d_f = p_f.shape[1]
    d_v = p_a.shape[1]
    n_heads = d_v // v_dim
    group_dim = d_f // N_FOURIER_GROUPS
    tn = d_f
    assert w_in.shape[2] == 4 * tn + 2 * d and d_v == tn and n_heads * 2 * head_dim == tn
    assert 2 * head_dim == V7X_LANES and v_dim == V7X_LANES

    m = batch * seq
    tm_ffn = min(512, seq)
    tm_in = min(512, seq)
    tm_merge = min(256, seq)
    t_dft = min(1024, seq)
    t_fold = min(512, seq // 2)
    tq = min(1024, seq)
    tk = min(2048, seq)
    tf = 512

    cmat, smat, wc = _dft_tables(seq, group_dim)
    rc, rs1, rs2 = _rope_lane_tables(seq, head_dim)
    blk = jnp.arange(V7X_MXU_DIM, dtype=jnp.int32) // head_dim
    e = (blk[:, None] == blk[None, :]).astype(BF16)
    q_scale = head_dim ** -0.5 * LOG2E

    ffa_gate, ffa_up, ffa_down, w_in, p_f, p_a, w_o, ffb_gate, ffb_up, ffb_down = (
        w.astype(BF16) for w in (ffa_gate, ffa_up, ffa_down, w_in, p_f, p_a, w_o, ffb_gate, ffb_up,
                                 ffb_down))
    xs = x.reshape(m, d)
    for i in range(depth):
        lam_init = 0.8 - 0.6 * math.exp(-0.3 * i)
        xs = _ffn(xs, norm_ffa[i][None], ffa_gate, ffa_up, ffa_down, norm_ffa[i][None], layer=i,
                  final_norm=False, tm=tm_ffn, tf=tf)
        qg = jnp.tile(q_norm[i], tn // head_dim)[None] * q_scale
        kg = jnp.tile(k_norm[i], tn // head_dim)[None]
        ab, qT, k, vT, sg = _inproj(xs, norm_mix[i][None], w_in, wc, e, qg, kg, rc, rs1, rs2,
                                    layer=i, batch=batch, seq=seq, tm=tm_in, tn=tn,
                                    group_dim=group_dim, head_dim=head_dim)
        folded = _fold(ab.reshape(batch, seq, 2 * tn), t=t_fold)
        f = _fourier(cmat, smat, folded, tm=t_dft, tk=min(t_dft, seq // 2))
        score_bound = head_dim * q_scale * jnp.max(jnp.abs(q_norm[i])) * jnp.max(jnp.abs(k_norm[i]))
        bounded = (score_bound <= SCORE_BOUND_LIMIT).astype(jnp.int32).reshape(1)
        o = _attention(bounded, qT, k.reshape(batch, seq, tn), vT, lambda_q1[i][None],
                       lambda_k1[i][None], lambda_q2[i][None], lambda_k2[i][None],
                       subln[i][:, None], lam_init=lam_init, n_heads=n_heads, tq=tq, tk=tk)
        xs = _merge(xs, f.reshape(m, d_f), o.reshape(m, d_v), sg, p_f, p_a, w_o, layer=i,
                    tm=tm_merge)
        xs = _ffn(xs, norm_ffb[i][None], ffb_gate, ffb_up, ffb_down, norm_out[i][None], layer=i,
                  final_norm=True, tm=tm_ffn, tf=tf)
    return xs.reshape(batch, seq, d)
```

```python
import functools
import math

import jax
import jax.numpy as jnp
from jax import lax
from jax.experimental import pallas as pl
from jax.experimental.pallas import tpu as pltpu

F32 = jnp.float32
BF16 = jnp.bfloat16

N_FOURIER_GROUPS = 4
ROPE_THETA = 500000.0
EPS = 1e-6
LOG2E = 1.4426950408889634
NEG_BIG = -1e30
SCORE_BOUND_LIMIT = 64.0

V7X_LANES = 128
V7X_MXU_DIM = 256
V7X_VMEM_LIMIT_BYTES = 56 * 1024 * 1024


def _params(*semantics):
    return pltpu.CompilerParams(dimension_semantics=semantics,
                                vmem_limit_bytes=V7X_VMEM_LIMIT_BYTES)


def _rms_scale(x):
    return lax.rsqrt(jnp.mean(x * x, axis=-1, keepdims=True) + EPS)


def _ffn_kernel(x_ref, gin_ref, wg_ref, wu_ref, wd_prev_ref, wd_last_ref, gout_ref, o_ref, h_ref,
                a_ref, *, n_chunks, final_norm):
    j = pl.program_id(1)

    def gate_up():
        h = h_ref[...]
        return (jnp.dot(h, wg_ref[...], preferred_element_type=F32),
                jnp.dot(h, wu_ref[...], preferred_element_type=F32))

    def hidden(g, u):
        return ((0.5 * g) * jax.nn.sigmoid(g) * u).astype(BF16)

    def down_prev(slot):
        o_ref[...] += jnp.dot(a_ref[slot], wd_prev_ref[...], preferred_element_type=F32)

    @pl.when(j == 0)
    def _():
        x = x_ref[...]
        h_ref[...] = (x * _rms_scale(x) * gin_ref[...]).astype(BF16)
        o_ref[...] = x
        a_ref[0] = hidden(*gate_up())

    for parity in range(2):
        @pl.when(jnp.logical_and(jnp.logical_and(j > 0, j < n_chunks - 1), j % 2 == parity))
        def _(parity=parity):
            g, u = gate_up()
            down_prev(1 - parity)
            a_ref[parity] = hidden(g, u)

    @pl.when(j == n_chunks - 1)
    def _():
        g, u = gate_up()
        down_prev(n_chunks % 2)
        y = o_ref[...] + jnp.dot(hidden(g, u), wd_last_ref[...], preferred_element_type=F32)
        if final_norm:
            y = y * _rms_scale(y) * gout_ref[...]
        o_ref[...] = y


def _ffn(x, g_in, wg, wu, wd, g_out, *, layer, final_norm, tm, tf):
    m, d = x.shape
    f = wg.shape[2]
    n_chunks = f // tf
    assert n_chunks >= 2
    return pl.pallas_call(
        functools.partial(_ffn_kernel, n_chunks=n_chunks, final_norm=final_norm),
        grid=(m // tm, n_chunks),
        in_specs=[
            pl.BlockSpec((tm, d), lambda i, j: (i, 0)),
            pl.BlockSpec((1, d), lambda i, j: (0, 0)),
            pl.BlockSpec((None, d, tf), lambda i, j: (layer, 0, j)),
            pl.BlockSpec((None, d, tf), lambda i, j: (layer, 0, j)),
            pl.BlockSpec((None, tf, d), lambda i, j: (layer, jnp.maximum(j - 1, 0), 0)),
            pl.BlockSpec((None, tf, d), lambda i, j: (layer, n_chunks - 1, 0)),
            pl.BlockSpec((1, d), lambda i, j: (0, 0)),
        ],
        out_specs=pl.BlockSpec((tm, d), lambda i, j: (i, 0)),
        out_shape=jax.ShapeDtypeStruct((m, d), F32),
        scratch_shapes=[pltpu.VMEM((tm, d), BF16), pltpu.VMEM((2, tm, tf), BF16)],
        compiler_params=_params("parallel", "arbitrary"),
        name="ffn",
    )(x, g_in, wg, wu, wd, wd, g_out)


def _norm_rope(z, e_ref, gain, rc, rs1, rs2, head_dim):
    tn = z.shape[1]
    zz = (z * z).astype(BF16)
    half = head_dim // 8
    outs = []
    for c in range(tn // V7X_MXU_DIM):
        cols = slice(c * V7X_MXU_DIM, (c + 1) * V7X_MXU_DIM)
        ss = jnp.dot(zz[:, cols], e_ref[...], preferred_element_type=F32)
        y = z[:, cols] * lax.rsqrt(ss * (1.0 / head_dim) + EPS) * gain[:, cols]
        for hh in range(V7X_MXU_DIM // V7X_LANES):
            yc = y[:, hh * V7X_LANES:(hh + 1) * V7X_LANES]
            outs.append(yc * rc + pltpu.roll(yc, V7X_LANES - half, 1) * rs1
                        + pltpu.roll(yc, half, 1) * rs2)
    return jnp.concatenate(outs, axis=1)


def _inproj_kernel(x_ref, gn_ref, w_ref, wc_ref, e_ref, qg_ref, kg_ref, rc_ref, rs1_ref, rs2_ref,
                   ab_ref, qT_ref, k_ref, vT_ref, sg_ref, h_ref, z_ref, *, group_dim, head_dim,
                   n_blocks):
    j = pl.program_id(1)
    tn = w_ref.shape[1]

    def epilogue(block, z):
        if block == 0:
            u = z.astype(BF16)
            n_groups = tn // group_dim
            for g in range(n_groups):
                t = jnp.dot(u[:, g * group_dim:(g + 1) * group_dim], wc_ref[...],
                            preferred_element_type=F32)
                ab_ref[:, g * group_dim:(g + 1) * group_dim] = t[:, :group_dim].astype(BF16)
                ab_ref[:, (n_groups + g) * group_dim:(n_groups + g + 1) * group_dim] = (
                    t[:, group_dim:].astype(BF16))
        elif block == 1:
            y = _norm_rope(z, e_ref, qg_ref[...], rc_ref[...], rs1_ref[...], rs2_ref[...], head_dim)
            qT_ref[0] = y.T.astype(BF16)
        elif block == 2:
            y = _norm_rope(z, e_ref, kg_ref[...], rc_ref[...], rs1_ref[...], rs2_ref[...], head_dim)
            k_ref[...] = y.astype(BF16)
        elif block == 3:
            vT_ref[0] = z.T.astype(BF16)
        else:
            col = (block - 4) % 2 * tn
            sg_ref[:, col:col + tn] = jax.nn.sigmoid(z).astype(BF16)

    for step in range(n_blocks):
        @pl.when(j == step)
        def _(step=step):
            if step == 0:
                x = x_ref[...]
                h_ref[...] = (x * _rms_scale(x) * gn_ref[...]).astype(BF16)
            if step > 0:
                epilogue(step - 1, z_ref[(step - 1) % 2])
            z = jnp.dot(h_ref[...], w_ref[...], preferred_element_type=F32)
            if step < n_blocks - 1:
                z_ref[step % 2] = z
            else:
                epilogue(step, z)


def _inproj(x, g_norm, w_in, wc, e, qg, kg, rc, rs1, rs2, *, layer, batch, seq, tm, tn, group_dim,
            head_dim):
    m, d = x.shape
    n_blocks = w_in.shape[2] // tn
    n_gate_blocks = n_blocks - 4
    assert n_gate_blocks == 4
    n_s = seq // tm
    n_tiles = m // tm
    const = lambda i, j: (0, 0)

    def tile_after(done_step):
        return lambda i, j: jnp.minimum(i + jnp.where(j > done_step, 1, 0), n_tiles - 1)

    t_ab, t_q, t_k, t_v = (tile_after(step) for step in (1, 2, 3, 4))
    return pl.pallas_call(
        functools.partial(_inproj_kernel, group_dim=group_dim, head_dim=head_dim,
                          n_blocks=n_blocks),
        grid=(n_tiles, n_blocks),
        in_specs=[
            pl.BlockSpec((tm, d), lambda i, j: (i, 0)),
            pl.BlockSpec((1, d), const),
            pl.BlockSpec((None, d, tn), lambda i, j: (layer, 0, j)),
            pl.BlockSpec(wc.shape, const),
            pl.BlockSpec(e.shape, const),
            pl.BlockSpec((1, tn), const),
            pl.BlockSpec((1, tn), const),
            pl.BlockSpec((tm, V7X_LANES), lambda i, j: (i % n_s, 0)),
            pl.BlockSpec((tm, V7X_LANES), lambda i, j: (i % n_s, 0)),
            pl.BlockSpec((tm, V7X_LANES), lambda i, j: (i % n_s, 0)),
        ],
        out_specs=[
            pl.BlockSpec((tm, 2 * tn), lambda i, j: (t_ab(i, j), 0)),
            pl.BlockSpec((1, tn, tm), lambda i, j: (t_q(i, j) // n_s, 0, t_q(i, j) % n_s)),
            pl.BlockSpec((tm, tn), lambda i, j: (t_k(i, j), 0)),
            pl.BlockSpec((1, tn, tm), lambda i, j: (t_v(i, j) // n_s, 0, t_v(i, j) % n_s)),
            pl.BlockSpec((tm, 2 * tn), lambda i, j: (i, jnp.where(j == n_blocks - 1, 1, 0))),
        ],
        out_shape=[
            jax.ShapeDtypeStruct((m, 2 * tn), BF16),
            jax.ShapeDtypeStruct((batch, tn, seq), BF16),
            jax.ShapeDtypeStruct((m, tn), BF16),
            jax.ShapeDtypeStruct((batch, tn, seq), BF16),
            jax.ShapeDtypeStruct((m, n_gate_blocks * tn), BF16),
        ],
        scratch_shapes=[pltpu.VMEM((tm, d), BF16), pltpu.VMEM((2, tm, tn), F32)],
        compiler_params=_params("arbitrary", "arbitrary"),
        name="inproj",
    )(x, g_norm, w_in, wc, e, qg, kg, rc, rs1, rs2)


def _fold_kernel(anti_ref, pick_ref, lo_ref, hi_ref, above_ref, mid_ref, o_ref):
    w = lo_ref.shape[2] // 2
    first = jnp.where(pl.program_id(0) == 0, 1.0, 0.0)
    for b in range(lo_ref.shape[0]):
        rev = (jnp.dot(anti_ref[...], hi_ref[b], preferred_element_type=F32)
               + jnp.dot(pick_ref[...], above_ref[b], preferred_element_type=F32))
        mid = jnp.dot(pick_ref[...], mid_ref[b], preferred_element_type=F32)
        lo = lo_ref[b].astype(F32)
        o_ref[b, :, :w] = (lo[:, :w] + rev[:, :w]).astype(BF16)
        o_ref[b, :, w:] = (lo[:, w:] - rev[:, w:] + first * mid[:, :w]).astype(BF16)


def _fold(ab, *, t):
    batch, seq, two_w = ab.shape
    nb = seq // t
    sub = 16
    r = jnp.arange(t, dtype=jnp.int32)
    anti = (r[:, None] + r[None, :] == t).astype(BF16)
    pick = (r[:, None] + jnp.arange(sub, dtype=jnp.int32)[None, :] == 0).astype(BF16)
    return pl.pallas_call(
        _fold_kernel,
        grid=(nb // 2,),
        in_specs=[
            pl.BlockSpec((t, t), lambda i: (0, 0)),
            pl.BlockSpec((t, sub), lambda i: (0, 0)),
            pl.BlockSpec((batch, t, two_w), lambda i: (0, i, 0)),
            pl.BlockSpec((batch, t, two_w), lambda i: (0, nb - 1 - i, 0)),
            pl.BlockSpec((batch, sub, two_w), lambda i: (0, ((nb - i) * (t // sub)) % (seq // sub), 0)),
            pl.BlockSpec((batch, sub, two_w), lambda i: (0, seq // 2 // sub, 0)),
        ],
        out_specs=pl.BlockSpec((batch, t, two_w), lambda i: (0, i, 0)),
        out_shape=jax.ShapeDtypeStruct((batch, seq // 2, two_w), BF16),
        compiler_params=_params("parallel"),
        name="fold",
    )(anti, pick, ab, ab, ab, ab)


def _fourier_kernel(c_ref, s_ref, a_ref, b_ref, o_ref, acc_ref):
    kk = pl.program_id(1)

    @pl.when(kk == 0)
    def _():
        acc_ref[...] = jnp.zeros_like(acc_ref)

    for b in range(a_ref.shape[0]):
        acc_ref[b] += (jnp.dot(c_ref[...], a_ref[b], preferred_element_type=F32)
                       + jnp.dot(s_ref[...], b_ref[b], preferred_element_type=F32))

    @pl.when(kk == pl.num_programs(1) - 1)
    def _():
        o_ref[...] = acc_ref[...].astype(o_ref.dtype)


def _fourier(cmat, smat, folded, *, tm, tk):
    batch, half_seq, two_w = folded.shape
    seq = cmat.shape[0]
    w = two_w // 2
    return pl.pallas_call(
        _fourier_kernel,
        grid=(seq // tm, half_seq // tk),
        in_specs=[
            pl.BlockSpec((tm, tk), lambda i, kk: (i, kk)),
            pl.BlockSpec((tm, tk), lambda i, kk: (i, kk)),
            pl.BlockSpec((batch, tk, w), lambda i, kk: (0, kk, 0)),
            pl.BlockSpec((batch, tk, w), lambda i, kk: (0, kk, 1)),
        ],
        out_specs=pl.BlockSpec((batch, tm, w), lambda i, kk: (0, i, 0)),
        out_shape=jax.ShapeDtypeStruct((batch, seq, w), BF16),
        scratch_shapes=[pltpu.VMEM((batch, tm, w), F32)],
        compiler_params=_params("parallel", "arbitrary"),
        name="fourier",
    )(cmat, smat, folded, folded)


def _attn_kernel(bounded_ref, qT_ref, k_ref, vT_ref, lq1_ref, lk1_ref, lq2_ref, lk2_ref, sub_ref,
                 o_ref, m_ref, l_ref, acc_ref, *, tk, lam_init):
    qT = qT_ref[0]
    qk_dim = qT.shape[0] // 2
    tq = qT.shape[1]
    row = lax.broadcasted_iota(jnp.int32, qT.shape, 0)
    zero = jnp.zeros_like(qT)
    q2 = jnp.concatenate([jnp.where(row < qk_dim, qT, zero), jnp.where(row >= qk_dim, qT, zero)],
                         axis=1)
    n_kb = k_ref.shape[1] // tk

    l_ref[...] = jnp.zeros_like(l_ref)
    acc_ref[...] = jnp.zeros_like(acc_ref)

    def blocks(j):
        start = pl.multiple_of(j * tk, tk)
        return k_ref[0, pl.ds(start, tk), :], vT_ref[0, :, pl.ds(start, tk)]

    @pl.when(bounded_ref[0] == 1)
    def _():
        def body(j, carry):
            kb, vb = blocks(j)
            s = jnp.dot(kb, q2, preferred_element_type=F32)
            p = jnp.exp2(s)
            l_ref[...] += jnp.sum(p.reshape(tk // 8, 8, 2 * tq), axis=0)
            acc_ref[...] += jnp.dot(vb, p.astype(BF16), preferred_element_type=F32)
            return carry

        lax.fori_loop(0, n_kb, body, 0)

    @pl.when(bounded_ref[0] != 1)
    def _():
        m_ref[...] = jnp.full_like(m_ref, NEG_BIG)

        def body(j, carry):
            kb, vb = blocks(j)
            s = jnp.dot(kb, q2, preferred_element_type=F32)
            m_prev = m_ref[...]
            m_new = jnp.maximum(m_prev, jnp.max(s, axis=0, keepdims=True))
            alpha = jnp.exp2(m_prev - m_new)
            p = jnp.exp2(s - m_new)
            l_ref[...] = alpha * l_ref[...] + jnp.sum(p.reshape(tk // 8, 8, 2 * tq), axis=0)
            acc_ref[...] = acc_ref[...] * alpha + jnp.dot(vb, p.astype(BF16),
                                                          preferred_element_type=F32)
            m_ref[...] = m_new
            return carry

        lax.fori_loop(0, n_kb, body, 0)

    lam = (jnp.exp(jnp.sum(lq1_ref[...] * lk1_ref[...], axis=-1, keepdims=True))
           - jnp.exp(jnp.sum(lq2_ref[...] * lk2_ref[...], axis=-1, keepdims=True)) + lam_init)
    l = jnp.sum(l_ref[...], axis=0, keepdims=True)
    o = acc_ref[:, :tq] * (1.0 / l[:, :tq]) - acc_ref[:, tq:] * (lam / l[:, tq:])
    scale = lax.rsqrt(jnp.mean(o * o, axis=0, keepdims=True) + EPS)
    y = o * scale * (sub_ref[...] * (1.0 - lam_init))
    o_ref[0] = y.T.astype(o_ref.dtype)


def _attention(bounded, qT, k, vT, lq1, lk1, lq2, lk2, sub, *, lam_init, n_heads, tq, tk):
    batch, d_qk, seq = qT.shape
    d_v = vT.shape[1]
    hq, hv = d_qk // n_heads, d_v // n_heads
    vec = lambda b, h, i, flag: (0, 0)
    return pl.pallas_call(
        functools.partial(_attn_kernel, tk=tk, lam_init=lam_init),
        grid_spec=pltpu.PrefetchScalarGridSpec(
            num_scalar_prefetch=1,
            grid=(batch, n_heads, seq // tq),
            in_specs=[
                pl.BlockSpec((1, hq, tq), lambda b, h, i, flag: (b, h, i)),
                pl.BlockSpec((1, seq, hq), lambda b, h, i, flag: (b, 0, h)),
                pl.BlockSpec((1, hv, seq), lambda b, h, i, flag: (b, h, 0)),
                pl.BlockSpec(lq1.shape, vec),
                pl.BlockSpec(lk1.shape, vec),
                pl.BlockSpec(lq2.shape, vec),
                pl.BlockSpec(lk2.shape, vec),
                pl.BlockSpec(sub.shape, vec),
            ],
            out_specs=pl.BlockSpec((1, tq, hv), lambda b, h, i, flag: (b, i, h)),
            scratch_shapes=[
                pltpu.VMEM((1, 2 * tq), F32),
                pltpu.VMEM((8, 2 * tq), F32),
                pltpu.VMEM((hv, 2 * tq), F32),
            ],
        ),
        out_shape=jax.ShapeDtypeStruct((batch, seq, d_v), BF16),
        compiler_params=_params("parallel", "parallel", "arbitrary"),
        name="attn",
    )(bounded, qT, k, vT, lq1, lk1, lq2, lk2, sub)


def _merge_kernel(x_ref, f_ref, o_ref, sg_ref, pf_ref, pa_ref, wo_ref, out_ref):
    d = x_ref.shape[1]
    bf = jnp.dot(f_ref[...], pf_ref[...], preferred_element_type=F32)
    ba = jnp.dot(o_ref[...], pa_ref[...], preferred_element_type=F32)
    mix = sg_ref[:, :d].astype(F32) * bf + sg_ref[:, d:].astype(F32) * ba
    out_ref[...] = x_ref[...] + jnp.dot(mix.astype(BF16), wo_ref[...], preferred_element_type=F32)


def _merge(x, f, o, sg, p_f, p_a, w_o, *, layer, tm):
    m, d = x.shape
    row = lambda i: (i, 0)
    stacked = lambda w: pl.BlockSpec((None,) + w.shape[1:], lambda i: (layer, 0, 0))
    return pl.pallas_call(
        _merge_kernel,
        grid=(m // tm,),
        in_specs=[
            pl.BlockSpec((tm, d), row),
            pl.BlockSpec((tm, f.shape[1]), row),
            pl.BlockSpec((tm, o.shape[1]), row),
            pl.BlockSpec((tm, sg.shape[1]), row),
            stacked(p_f),
            stacked(p_a),
            stacked(w_o),
        ],
        out_specs=pl.BlockSpec((tm, d), row),
        out_shape=jax.ShapeDtypeStruct((m, d), F32),
        compiler_params=_params("parallel"),
        name="merge",
    )(x, f, o, sg, p_f, p_a, w_o)


def _dft_tables(seq, group_dim):
    s = jnp.arange(seq // 2, dtype=jnp.int32)[None, :]
    hi = jnp.arange(seq // V7X_LANES, dtype=jnp.int32)[:, None]
    lo = jnp.arange(V7X_LANES, dtype=jnp.int32)[:, None]
    ang_hi = ((hi * V7X_LANES * s) % seq).astype(F32) * (2.0 * math.pi / seq)
    ang_lo = ((lo * s) % seq).astype(F32) * (2.0 * math.pi / seq)
    ca, sa = jnp.cos(ang_hi)[:, None, :], jnp.sin(ang_hi)[:, None, :]
    cb, sb = jnp.cos(ang_lo)[None, :, :], jnp.sin(ang_lo)[None, :, :]
    k_par = (1 - 2 * (jnp.arange(seq, dtype=jnp.int32) % 2)).astype(F32)[:, None]
    col0 = s == 0
    cmat = jnp.where(col0, 0.5, (ca * cb - sa * sb).reshape(seq, seq // 2)).astype(BF16)
    smat = jnp.where(col0, k_par, (-(sa * cb + ca * sb)).reshape(seq, seq // 2)).astype(BF16)
    c = jnp.arange(group_dim, dtype=jnp.int32)
    angc = ((c[:, None] * c[None, :]) % group_dim).astype(F32) * (2.0 * math.pi / group_dim)
    ortho = 1.0 / math.sqrt(seq * group_dim)
    wc = (jnp.concatenate([jnp.cos(angc), jnp.sin(angc)], axis=1) * ortho).astype(BF16)
    return cmat, smat, wc


def _rope_lane_tables(seq, head_dim):
    rope_dim = head_dim // 4
    half = rope_dim // 2
    pos = jnp.arange(seq, dtype=F32)
    inv_freq = ROPE_THETA ** (-jnp.arange(0, rope_dim, 2, dtype=F32) / rope_dim)
    ang = pos[:, None] * inv_freq[None, :]
    cos, sin = jnp.cos(ang), jnp.sin(ang)
    ones = jnp.ones((seq, head_dim - rope_dim), F32)
    zeros_h = jnp.zeros((seq, half), F32)
    zeros_r = jnp.zeros((seq, head_dim - rope_dim), F32)
    rc = jnp.concatenate([cos, cos, ones], axis=1)
    rs1 = jnp.concatenate([-sin, zeros_h, zeros_r], axis=1)
    rs2 = jnp.concatenate([zeros_h, sin, zeros_r], axis=1)
    reps = V7X_LANES // head_dim
    return tuple(jnp.tile(t, (1, reps)) for t in (rc, rs1, rs2))


def kernel(x, norm_ffa, ffa_gate, ffa_up, ffa_down, norm_mix, w_in, q_norm, k_norm, lambda_q1,
           lambda_k1, lambda_q2, lambda_k2, subln, p_f, p_a, w_o, norm_ffb, ffb_gate, ffb_up,
           ffb_down, norm_out):
    batch, seq, d = x.shape
    depth = w_in.shape[0]
    head_dim = q_norm.shape[1]
    v_dim = subln.shape[1]
    d_f = p_f.shape[1]
    d_v = p_a.shape[1]
    n_heads = d_v // v_dim
    group_dim = d_f // N_FOURIER_GROUPS
    tn = d_f
    assert w_in.shape[2] == 4 * tn + 2 * d and d_v == tn and n_heads * 2 * head_dim == tn
    assert 2 * head_dim == V7X_LANES and v_dim == V7X_LANES

    m = batch * seq
    tm_ffn = min(512, seq)
    tm_in = min(512, seq)
    tm_merge = min(256, seq)
    t_dft = min(1024, seq)
    t_fold = min(512, seq // 2)
    tq = min(1024, seq)
    tk = min(2048, seq)
    tf = 512

    cmat, smat, wc = _dft_tables(seq, group_dim)
    rc, rs1, rs2 = _rope_lane_tables(seq, head_dim)
    blk = jnp.arange(V7X_MXU_DIM, dtype=jnp.int32) // head_dim
    e = (blk[:, None] == blk[None, :]).astype(BF16)
    q_scale = head_dim ** -0.5 * LOG2E

    ffa_gate, ffa_up, ffa_down, w_in, p_f, p_a, w_o, ffb_gate, ffb_up, ffb_down = (
        w.astype(BF16) for w in (ffa_gate, ffa_up, ffa_down, w_in, p_f, p_a, w_o, ffb_gate, ffb_up,
                                 ffb_down))
    xs = x.reshape(m, d)
    for i in range(depth):
        lam_init = 0.8 - 0.6 * math.exp(-0.3 * i)
        xs = _ffn(xs, norm_ffa[i][None], ffa_gate, ffa_up, ffa_down, norm_ffa[i][None], layer=i,
                  final_norm=False, tm=tm_ffn, tf=tf)
        qg = jnp.tile(q_norm[i], tn // head_dim)[None] * q_scale
        kg = jnp.tile(k_norm[i], tn // head_dim)[None]
        ab, qT, k, vT, sg = _inproj(xs, norm_mix[i][None], w_in, wc, e, qg, kg, rc, rs1, rs2,
                                    layer=i, batch=batch, seq=seq, tm=tm_in, tn=tn,
                                    group_dim=group_dim, head_dim=head_dim)
        folded = _fold(ab.reshape(batch, seq, 2 * tn), t=t_fold)
        f = _fourier(cmat, smat, folded, tm=t_dft, tk=min(t_dft, seq // 2))
        score_bound = head_dim * q_scale * jnp.max(jnp.abs(q_norm[i])) * jnp.max(jnp.abs(k_norm[i]))
        bounded = (score_bound <= SCORE_BOUND_LIMIT).astype(jnp.int32).reshape(1)
        o = _attention(bounded, qT, k.reshape(batch, seq, tn), vT, lambda_q1[i][None],
                       lambda_k1[i][None], lambda_q2[i][None], lambda_k2[i][None],
                       subln[i][:, None], lam_init=lam_init, n_heads=n_heads, tq=tq, tk=tk)
        xs = _merge(xs, f.reshape(m, d_f), o.reshape(m, d_v), sg, p_f, p_a, w_o, layer=i,
                    tm=tm_merge)
        xs = _ffn(xs, norm_ffb[i][None], ffb_gate, ffb_up, ffb_down, norm_out[i][None], layer=i,
                  final_norm=True, tm=tm_ffn, tf=tf)
    return xs.reshape(batch, seq, d)
```

```python
import functools
import math

import jax
import jax.numpy as jnp
from jax import lax
from jax.experimental import pallas as pl
from jax.experimental.pallas import tpu as pltpu

F32 = jnp.float32
BF16 = jnp.bfloat16

N_FOURIER_GROUPS = 4
ROPE_THETA = 500000.0
EPS = 1e-6
LOG2E = 1.4426950408889634
NEG_BIG = -1e30
SCORE_BOUND_LIMIT = 64.0

V7X_LANES = 128
V7X_MXU_DIM = 256
V7X_VMEM_LIMIT_BYTES = 56 * 1024 * 1024


def _params(*semantics):
    return pltpu.CompilerParams(dimension_semantics=semantics,
                                vmem_limit_bytes=V7X_VMEM_LIMIT_BYTES)


def _rms_scale(x):
    return lax.rsqrt(jnp.mean(x * x, axis=-1, keepdims=True) + EPS)


def _ffn_kernel(x_ref, gin_ref, wg_ref, wu_ref, wd_ref, gout_ref, o_ref, h_ref, *, final_norm):
    j = pl.program_id(1)

    @pl.when(j == 0)
    def _():
        x = x_ref[...]
        h_ref[...] = (x * _rms_scale(x) * gin_ref[...]).astype(BF16)
        o_ref[...] = x

    h = h_ref[...]
    g = jnp.dot(h, wg_ref[...], preferred_element_type=F32)
    u = jnp.dot(h, wu_ref[...], preferred_element_type=F32)
    a = (0.5 * g) * jax.nn.sigmoid(g) * u
    o_ref[...] += jnp.dot(a.astype(BF16), wd_ref[...], preferred_element_type=F32)

    if final_norm:
        @pl.when(j == pl.num_programs(1) - 1)
        def _():
            y = o_ref[...]
            o_ref[...] = y * _rms_scale(y) * gout_ref[...]


def _ffn(x, g_in, wg, wu, wd, g_out, *, layer, final_norm, tm):
    m, d = x.shape
    n_chunks, tf = wg.shape[1], wg.shape[3]
    return pl.pallas_call(
        functools.partial(_ffn_kernel, final_norm=final_norm),
        grid=(m // tm, n_chunks),
        in_specs=[
            pl.BlockSpec((tm, d), lambda i, j: (i, 0)),
            pl.BlockSpec((1, d), lambda i, j: (0, 0)),
            pl.BlockSpec((None, None, d, tf), lambda i, j: (layer, j, 0, 0)),
            pl.BlockSpec((None, None, d, tf), lambda i, j: (layer, j, 0, 0)),
            pl.BlockSpec((None, tf, d), lambda i, j: (layer, j, 0)),
            pl.BlockSpec((1, d), lambda i, j: (0, 0)),
        ],
        out_specs=pl.BlockSpec((tm, d), lambda i, j: (i, 0)),
        out_shape=jax.ShapeDtypeStruct((m, d), F32),
        scratch_shapes=[pltpu.VMEM((tm, d), BF16)],
        compiler_params=_params("parallel", "arbitrary"),
        name="ffn",
    )(x, g_in, wg, wu, wd, g_out)


def _norm_rope(z, e_ref, gain, rc, rs1, rs2, head_dim):
    tn = z.shape[1]
    zz = (z * z).astype(BF16)
    half = head_dim // 8
    outs = []
    for c in range(tn // V7X_MXU_DIM):
        cols = slice(c * V7X_MXU_DIM, (c + 1) * V7X_MXU_DIM)
        ss = jnp.dot(zz[:, cols], e_ref[...], preferred_element_type=F32)
        y = z[:, cols] * lax.rsqrt(ss * (1.0 / head_dim) + EPS) * gain[:, cols]
        for hh in range(V7X_MXU_DIM // V7X_LANES):
            yc = y[:, hh * V7X_LANES:(hh + 1) * V7X_LANES]
            outs.append(yc * rc + pltpu.roll(yc, V7X_LANES - half, 1) * rs1
                        + pltpu.roll(yc, half, 1) * rs2)
    return jnp.concatenate(outs, axis=1)


def _inproj_kernel(x_ref, gn_ref, w_ref, wc_ref, e_ref, qg_ref, kg_ref, rc_ref, rs1_ref, rs2_ref,
                   ab_ref, qT_ref, k_ref, vT_ref, sg_ref, h_ref, z_ref, *, group_dim, head_dim,
                   n_blocks):
    j = pl.program_id(1)
    tn = w_ref.shape[1]

    def epilogue(block, z):
        if block == 0:
            u = z.astype(BF16)
            n_groups = tn // group_dim
            for g in range(n_groups):
                t = jnp.dot(u[:, g * group_dim:(g + 1) * group_dim], wc_ref[...],
                            preferred_element_type=F32)
                ab_ref[:, g * group_dim:(g + 1) * group_dim] = t[:, :group_dim].astype(BF16)
                ab_ref[:, (n_groups + g) * group_dim:(n_groups + g + 1) * group_dim] = (
                    t[:, group_dim:].astype(BF16))
        elif block == 1:
            y = _norm_rope(z, e_ref, qg_ref[...], rc_ref[...], rs1_ref[...], rs2_ref[...], head_dim)
            qT_ref[0] = y.T.astype(BF16)
        elif block == 2:
            y = _norm_rope(z, e_ref, kg_ref[...], rc_ref[...], rs1_ref[...], rs2_ref[...], head_dim)
            k_ref[...] = y.astype(BF16)
        elif block == 3:
            vT_ref[0] = z.T.astype(BF16)
        else:
            col = (block - 4) % 2 * tn
            sg_ref[:, col:col + tn] = jax.nn.sigmoid(z).astype(BF16)

    for step in range(n_blocks):
        @pl.when(j == step)
        def _(step=step):
            if step == 0:
                x = x_ref[...]
                h_ref[...] = (x * _rms_scale(x) * gn_ref[...]).astype(BF16)
            if step > 0:
                epilogue(step - 1, z_ref[(step - 1) % 2])
            z = jnp.dot(h_ref[...], w_ref[...], preferred_element_type=F32)
            if step < n_blocks - 1:
                z_ref[step % 2] = z
            else:
                epilogue(step, z)


def _inproj(x, g_norm, w_in, wc, e, qg, kg, rc, rs1, rs2, *, layer, batch, seq, tm, tn, group_dim,
            head_dim):
    m, d = x.shape
    n_blocks = w_in.shape[1]
    n_gate_blocks = n_blocks - 4
    assert n_gate_blocks == 4
    n_s = seq // tm
    n_tiles = m // tm
    const = lambda i, j: (0, 0)

    def tile_after(done_step):
        return lambda i, j: jnp.minimum(i + jnp.where(j > done_step, 1, 0), n_tiles - 1)

    t_ab, t_q, t_k, t_v = (tile_after(step) for step in (1, 2, 3, 4))
    return pl.pallas_call(
        functools.partial(_inproj_kernel, group_dim=group_dim, head_dim=head_dim,
                          n_blocks=n_blocks),
        grid=(n_tiles, n_blocks),
        in_specs=[
            pl.BlockSpec((tm, d), lambda i, j: (i, 0)),
            pl.BlockSpec((1, d), const),
            pl.BlockSpec((None, None, d, tn), lambda i, j: (layer, j, 0, 0)),
            pl.BlockSpec(wc.shape, const),
            pl.BlockSpec(e.shape, const),
            pl.BlockSpec((1, tn), const),
            pl.BlockSpec((1, tn), const),
            pl.BlockSpec((tm, V7X_LANES), lambda i, j: (i % n_s, 0)),
            pl.BlockSpec((tm, V7X_LANES), lambda i, j: (i % n_s, 0)),
            pl.BlockSpec((tm, V7X_LANES), lambda i, j: (i % n_s, 0)),
        ],
        out_specs=[
            pl.BlockSpec((tm, 2 * tn), lambda i, j: (t_ab(i, j), 0)),
            pl.BlockSpec((1, tn, tm), lambda i, j: (t_q(i, j) // n_s, 0, t_q(i, j) % n_s)),
            pl.BlockSpec((tm, tn), lambda i, j: (t_k(i, j), 0)),
            pl.BlockSpec((1, tn, tm), lambda i, j: (t_v(i, j) // n_s, 0, t_v(i, j) % n_s)),
            pl.BlockSpec((tm, 2 * tn), lambda i, j: (i, jnp.where(j == n_blocks - 1, 1, 0))),
        ],
        out_shape=[
            jax.ShapeDtypeStruct((m, 2 * tn), BF16),
            jax.ShapeDtypeStruct((batch, tn, seq), BF16),
            jax.ShapeDtypeStruct((m, tn), BF16),
            jax.ShapeDtypeStruct((batch, tn, seq), BF16),
            jax.ShapeDtypeStruct((m, n_gate_blocks * tn), BF16),
        ],
        scratch_shapes=[pltpu.VMEM((tm, d), BF16), pltpu.VMEM((2, tm, tn), F32)],
        compiler_params=_params("arbitrary", "arbitrary"),
        name="inproj",
    )(x, g_norm, w_in, wc, e, qg, kg, rc, rs1, rs2)


def _fold_kernel(anti_ref, pick_ref, lo_ref, hi_ref, above_ref, mid_ref, o_ref):
    w = lo_ref.shape[2] // 2
    first = jnp.where(pl.program_id(0) == 0, 1.0, 0.0)
    for b in range(lo_ref.shape[0]):
        rev = (jnp.dot(anti_ref[...], hi_ref[b], preferred_element_type=F32)
               + jnp.dot(pick_ref[...], above_ref[b], preferred_element_type=F32))
        mid = jnp.dot(pick_ref[...], mid_ref[b], preferred_element_type=F32)
        lo = lo_ref[b].astype(F32)
        o_ref[b, :, :w] = (lo[:, :w] + rev[:, :w]).astype(BF16)
        o_ref[b, :, w:] = (lo[:, w:] - rev[:, w:] + first * mid[:, :w]).astype(BF16)


def _fold(ab, *, t):
    batch, seq, two_w = ab.shape
    nb = seq // t
    sub = 16
    r = jnp.arange(t, dtype=jnp.int32)
    anti = (r[:, None] + r[None, :] == t).astype(BF16)
    pick = (r[:, None] + jnp.arange(sub, dtype=jnp.int32)[None, :] == 0).astype(BF16)
    return pl.pallas_call(
        _fold_kernel,
        grid=(nb // 2,),
        in_specs=[
            pl.BlockSpec((t, t), lambda i: (0, 0)),
            pl.BlockSpec((t, sub), lambda i: (0, 0)),
            pl.BlockSpec((batch, t, two_w), lambda i: (0, i, 0)),
            pl.BlockSpec((batch, t, two_w), lambda i: (0, nb - 1 - i, 0)),
            pl.BlockSpec((batch, sub, two_w), lambda i: (0, ((nb - i) * (t // sub)) % (seq // sub), 0)),
            pl.BlockSpec((batch, sub, two_w), lambda i: (0, seq // 2 // sub, 0)),
        ],
        out_specs=pl.BlockSpec((batch, t, two_w), lambda i: (0, i, 0)),
        out_shape=jax.ShapeDtypeStruct((batch, seq // 2, two_w), BF16),
        compiler_params=_params("parallel"),
        name="fold",
    )(anti, pick, ab, ab, ab, ab)


def _fourier_kernel(c_ref, s_ref, a_ref, b_ref, o_ref, acc_ref):
    kk = pl.program_id(1)

    @pl.when(kk == 0)
    def _():
        acc_ref[...] = jnp.zeros_like(acc_ref)

    for b in range(a_ref.shape[0]):
        acc_ref[b] += (jnp.dot(c_ref[...], a_ref[b], preferred_element_type=F32)
                       + jnp.dot(s_ref[...], b_ref[b], preferred_element_type=F32))

    @pl.when(kk == pl.num_programs(1) - 1)
    def _():
        o_ref[...] = acc_ref[...].astype(o_ref.dtype)


def _fourier(cmat, smat, folded, *, tm, tk):
    batch, half_seq, two_w = folded.shape
    seq = cmat.shape[0]
    w = two_w // 2
    return pl.pallas_call(
        _fourier_kernel,
        grid=(seq // tm, half_seq // tk),
        in_specs=[
            pl.BlockSpec((tm, tk), lambda i, kk: (i, kk)),
            pl.BlockSpec((tm, tk), lambda i, kk: (i, kk)),
            pl.BlockSpec((batch, tk, w), lambda i, kk: (0, kk, 0)),
            pl.BlockSpec((batch, tk, w), lambda i, kk: (0, kk, 1)),
        ],
        out_specs=pl.BlockSpec((batch, tm, w), lambda i, kk: (0, i, 0)),
        out_shape=jax.ShapeDtypeStruct((batch, seq, w), BF16),
        scratch_shapes=[pltpu.VMEM((batch, tm, w), F32)],
        compiler_params=_params("parallel", "arbitrary"),
        name="fourier",
    )(cmat, smat, folded, folded)


def _attn_kernel(bounded_ref, qT_ref, k_ref, vT_ref, lq1_ref, lk1_ref, lq2_ref, lk2_ref, sub_ref,
                 o_ref, m_ref, l_ref, acc_ref, *, tk, lam_init):
    qT = qT_ref[0]
    qk_dim = qT.shape[0] // 2
    tq = qT.shape[1]
    row = lax.broadcasted_iota(jnp.int32, qT.shape, 0)
    zero = jnp.zeros_like(qT)
    q2 = jnp.concatenate([jnp.where(row < qk_dim, qT, zero), jnp.where(row >= qk_dim, qT, zero)],
                         axis=1)
    n_kb = k_ref.shape[1] // tk

    l_ref[...] = jnp.zeros_like(l_ref)
    acc_ref[...] = jnp.zeros_like(acc_ref)

    def blocks(j):
        start = pl.multiple_of(j * tk, tk)
        return k_ref[0, pl.ds(start, tk), :], vT_ref[0, :, pl.ds(start, tk)]

    @pl.when(bounded_ref[0] == 1)
    def _():
        def body(j, carry):
            kb, vb = blocks(j)
            s = jnp.dot(kb, q2, preferred_element_type=F32)
            p = jnp.exp2(s)
            l_ref[...] += jnp.sum(p.reshape(tk // 8, 8, 2 * tq), axis=0)
            acc_ref[...] += jnp.dot(vb, p.astype(BF16), preferred_element_type=F32)
            return carry

        lax.fori_loop(0, n_kb, body, 0)

    @pl.when(bounded_ref[0] != 1)
    def _():
        m_ref[...] = jnp.full_like(m_ref, NEG_BIG)

        def body(j, carry):
            kb, vb = blocks(j)
            s = jnp.dot(kb, q2, preferred_element_type=F32)
            m_prev = m_ref[...]
            m_new = jnp.maximum(m_prev, jnp.max(s, axis=0, keepdims=True))
            alpha = jnp.exp2(m_prev - m_new)
            p = jnp.exp2(s - m_new)
            l_ref[...] = alpha * l_ref[...] + jnp.sum(p.reshape(tk // 8, 8, 2 * tq), axis=0)
            acc_ref[...] = acc_ref[...] * alpha + jnp.dot(vb, p.astype(BF16),
                                                          preferred_element_type=F32)
            m_ref[...] = m_new
            return carry

        lax.fori_loop(0, n_kb, body, 0)

    lam = (jnp.exp(jnp.sum(lq1_ref[...] * lk1_ref[...], axis=-1, keepdims=True))
           - jnp.exp(jnp.sum(lq2_ref[...] * lk2_ref[...], axis=-1, keepdims=True)) + lam_init)
    l = jnp.sum(l_ref[...], axis=0, keepdims=True)
    o = acc_ref[:, :tq] * (1.0 / l[:, :tq]) - acc_ref[:, tq:] * (lam / l[:, tq:])
    scale = lax.rsqrt(jnp.mean(o * o, axis=0, keepdims=True) + EPS)
    y = o * scale * (sub_ref[...] * (1.0 - lam_init))
    o_ref[0] = y.T.astype(o_ref.dtype)


def _attention(bounded, qT, k, vT, lq1, lk1, lq2, lk2, sub, *, lam_init, n_heads, tq, tk):
    batch, d_qk, seq = qT.shape
    d_v = vT.shape[1]
    hq, hv = d_qk // n_heads, d_v // n_heads
    vec = lambda b, h, i, flag: (0, 0)
    return pl.pallas_call(
        functools.partial(_attn_kernel, tk=tk, lam_init=lam_init),
        grid_spec=pltpu.PrefetchScalarGridSpec(
            num_scalar_prefetch=1,
            grid=(batch, n_heads, seq // tq),
            in_specs=[
                pl.BlockSpec((1, hq, tq), lambda b, h, i, flag: (b, h, i)),
                pl.BlockSpec((1, seq, hq), lambda b, h, i, flag: (b, 0, h)),
                pl.BlockSpec((1, hv, seq), lambda b, h, i, flag: (b, h, 0)),
                pl.BlockSpec(lq1.shape, vec),
                pl.BlockSpec(lk1.shape, vec),
                pl.BlockSpec(lq2.shape, vec),
                pl.BlockSpec(lk2.shape, vec),
                pl.BlockSpec(sub.shape, vec),
            ],
            out_specs=pl.BlockSpec((1, tq, hv), lambda b, h, i, flag: (b, i, h)),
            scratch_shapes=[
                pltpu.VMEM((1, 2 * tq), F32),
                pltpu.VMEM((8, 2 * tq), F32),
                pltpu.VMEM((hv, 2 * tq), F32),
            ],
        ),
        out_shape=jax.ShapeDtypeStruct((batch, seq, d_v), BF16),
        compiler_params=_params("parallel", "parallel", "arbitrary"),
        name="attn",
    )(bounded, qT, k, vT, lq1, lk1, lq2, lk2, sub)


def _merge_kernel(x_ref, f_ref, o_ref, sg_ref, pf_ref, pa_ref, wo_ref, out_ref):
    d = x_ref.shape[1]
    bf = jnp.dot(f_ref[...], pf_ref[...], preferred_element_type=F32)
    ba = jnp.dot(o_ref[...], pa_ref[...], preferred_element_type=F32)
    mix = sg_ref[:, :d].astype(F32) * bf + sg_ref[:, d:].astype(F32) * ba
    out_ref[...] = x_ref[...] + jnp.dot(mix.astype(BF16), wo_ref[...], preferred_element_type=F32)


def _merge(x, f, o, sg, p_f, p_a, w_o, *, layer, tm):
    m, d = x.shape
    row = lambda i: (i, 0)
    stacked = lambda w: pl.BlockSpec((None,) + w.shape[1:], lambda i: (layer, 0, 0))
    return pl.pallas_call(
        _merge_kernel,
        grid=(m // tm,),
        in_specs=[
            pl.BlockSpec((tm, d), row),
            pl.BlockSpec((tm, f.shape[1]), row),
            pl.BlockSpec((tm, o.shape[1]), row),
            pl.BlockSpec((tm, sg.shape[1]), row),
            stacked(p_f),
            stacked(p_a),
            stacked(w_o),
        ],
        out_specs=pl.BlockSpec((tm, d), row),
        out_shape=jax.ShapeDtypeStruct((m, d), F32),
        compiler_params=_params("parallel"),
        name="merge",
    )(x, f, o, sg, p_f, p_a, w_o)


def _dft_tables(seq, group_dim):
    s = jnp.arange(seq // 2, dtype=jnp.int32)[None, :]
    hi = jnp.arange(seq // V7X_LANES, dtype=jnp.int32)[:, None]
    lo = jnp.arange(V7X_LANES, dtype=jnp.int32)[:, None]
    ang_hi = ((hi * V7X_LANES * s) % seq).astype(F32) * (2.0 * math.pi / seq)
    ang_lo = ((lo * s) % seq).astype(F32) * (2.0 * math.pi / seq)
    ca, sa = jnp.cos(ang_hi)[:, None, :], jnp.sin(ang_hi)[:, None, :]
    cb, sb = jnp.cos(ang_lo)[None, :, :], jnp.sin(ang_lo)[None, :, :]
    k_par = (1 - 2 * (jnp.arange(seq, dtype=jnp.int32) % 2)).astype(F32)[:, None]
    col0 = s == 0
    cmat = jnp.where(col0, 0.5, (ca * cb - sa * sb).reshape(seq, seq // 2)).astype(BF16)
    smat = jnp.where(col0, k_par, (-(sa * cb + ca * sb)).reshape(seq, seq // 2)).astype(BF16)
    c = jnp.arange(group_dim, dtype=jnp.int32)
    angc = ((c[:, None] * c[None, :]) % group_dim).astype(F32) * (2.0 * math.pi / group_dim)
    ortho = 1.0 / math.sqrt(seq * group_dim)
    wc = (jnp.concatenate([jnp.cos(angc), jnp.sin(angc)], axis=1) * ortho).astype(BF16)
    return cmat, smat, wc


def _rope_lane_tables(seq, head_dim):
    rope_dim = head_dim // 4
    half = rope_dim // 2
    pos = jnp.arange(seq, dtype=F32)
    inv_freq = ROPE_THETA ** (-jnp.arange(0, rope_dim, 2, dtype=F32) / rope_dim)
    ang = pos[:, None] * inv_freq[None, :]
    cos, sin = jnp.cos(ang), jnp.sin(ang)
    ones = jnp.ones((seq, head_dim - rope_dim), F32)
    zeros_h = jnp.zeros((seq, half), F32)
    zeros_r = jnp.zeros((seq, head_dim - rope_dim), F32)
    rc = jnp.concatenate([cos, cos, ones], axis=1)
    rs1 = jnp.concatenate([-sin, zeros_h, zeros_r], axis=1)
    rs2 = jnp.concatenate([zeros_h, sin, zeros_r], axis=1)
    reps = V7X_LANES // head_dim
    return tuple(jnp.tile(t, (1, reps)) for t in (rc, rs1, rs2))


def kernel(x, norm_ffa, ffa_gate, ffa_up, ffa_down, norm_mix, w_in, q_norm, k_norm, lambda_q1,
           lambda_k1, lambda_q2, lambda_k2, subln, p_f, p_a, w_o, norm_ffb, ffb_gate, ffb_up,
           ffb_down, norm_out):
    batch, seq, d = x.shape
    depth = w_in.shape[0]
    head_dim = q_norm.shape[1]
    v_dim = subln.shape[1]
    d_f = p_f.shape[1]
    d_v = p_a.shape[1]
    n_heads = d_v // v_dim
    group_dim = d_f // N_FOURIER_GROUPS
    tn = d_f
    assert w_in.shape[2] == 4 * tn + 2 * d and d_v == tn and n_heads * 2 * head_dim == tn
    assert 2 * head_dim == V7X_LANES and v_dim == V7X_LANES

    m = batch * seq
    tm_ffn = min(512, seq)
    tm_in = min(512, seq)
    tm_merge = min(256, seq)
    t_dft = min(1024, seq)
    t_fold = min(512, seq // 2)
    tq = min(1024, seq)
    tk = min(2048, seq)
    tf = 512

    cmat, smat, wc = _dft_tables(seq, group_dim)
    rc, rs1, rs2 = _rope_lane_tables(seq, head_dim)
    blk = jnp.arange(V7X_MXU_DIM, dtype=jnp.int32) // head_dim
    e = (blk[:, None] == blk[None, :]).astype(BF16)
    q_scale = head_dim ** -0.5 * LOG2E

    def col_blocks(w, t):
        return w.astype(BF16).reshape(depth, w.shape[1], w.shape[2] // t, t).transpose(0, 2, 1, 3)

    ffa_gate, ffa_up, ffb_gate, ffb_up = (col_blocks(w, tf) for w in (ffa_gate, ffa_up, ffb_gate,
                                                                      ffb_up))
    w_in = col_blocks(w_in, tn)
    ffa_down, ffb_down, p_f, p_a, w_o = (w.astype(BF16) for w in (ffa_down, ffb_down, p_f, p_a, w_o))
    xs = x.reshape(m, d)
    for i in range(depth):
        lam_init = 0.8 - 0.6 * math.exp(-0.3 * i)
        xs = _ffn(xs, norm_ffa[i][None], ffa_gate, ffa_up, ffa_down, norm_ffa[i][None], layer=i,
                  final_norm=False, tm=tm_ffn)
        qg = jnp.tile(q_norm[i], tn // head_dim)[None] * q_scale
        kg = jnp.tile(k_norm[i], tn // head_dim)[None]
        ab, qT, k, vT, sg = _inproj(xs, norm_mix[i][None], w_in, wc, e, qg, kg, rc, rs1, rs2,
                                    layer=i, batch=batch, seq=seq, tm=tm_in, tn=tn,
                                    group_dim=group_dim, head_dim=head_dim)
        folded = _fold(ab.reshape(batch, seq, 2 * tn), t=t_fold)
        f = _fourier(cmat, smat, folded, tm=t_dft, tk=min(t_dft, seq // 2))
        score_bound = head_dim * q_scale * jnp.max(jnp.abs(q_norm[i])) * jnp.max(jnp.abs(k_norm[i]))
        bounded = (score_bound <= SCORE_BOUND_LIMIT).astype(jnp.int32).reshape(1)
        o = _attention(bounded, qT, k.reshape(batch, seq, tn), vT, lambda_q1[i][None],
                       lambda_k1[i][None], lambda_q2[i][None], lambda_k2[i][None],
                       subln[i][:, None], lam_init=lam_init, n_heads=n_heads, tq=tq, tk=tk)
        xs = _merge(xs, f.reshape(m, d_f), o.reshape(m, d_v), sg, p_f, p_a, w_o, layer=i,
                    tm=tm_merge)
        xs = _ffn(xs, norm_ffb[i][None], ffb_gate, ffb_up, ffb_down, norm_out[i][None], layer=i,
                  final_norm=True, tm=tm_ffn)
    return xs.reshape(batch, seq, d)
```

```python
import functools
import math

import jax
import jax.numpy as jnp
from jax import lax
from jax.experimental import pallas as pl
from jax.experimental.pallas import tpu as pltpu

F32 = jnp.float32
BF16 = jnp.bfloat16

N_FOURIER_GROUPS = 4
ROPE_THETA = 500000.0
EPS = 1e-6
LOG2E = 1.4426950408889634
NEG_BIG = -1e30
SCORE_BOUND_LIMIT = 64.0

V7X_LANES = 128
V7X_MXU_DIM = 256
V7X_VMEM_LIMIT_BYTES = 60 * 1024 * 1024


def _params(*semantics):
    return pltpu.CompilerParams(dimension_semantics=semantics,
                                vmem_limit_bytes=V7X_VMEM_LIMIT_BYTES)


def _rms_scale(x):
    return lax.rsqrt(jnp.mean(x * x, axis=-1, keepdims=True) + EPS)


def _ffn_kernel(x_ref, gin_ref, wg_ref, wu_ref, wd_ref, gtail_ref, o_ref, h_ref, *, tail):
    j = pl.program_id(1)

    @pl.when(j == 0)
    def _():
        x = x_ref[...]
        h_ref[...] = (x * _rms_scale(x) * gin_ref[...]).astype(BF16)
        o_ref[...] = x

    h = h_ref[...]
    g = jnp.dot(h, wg_ref[...], preferred_element_type=F32)
    u = jnp.dot(h, wu_ref[...], preferred_element_type=F32)
    a = (0.5 * g) * jax.nn.sigmoid(g) * u
    o_ref[...] += jnp.dot(a.astype(BF16), wd_ref[...], preferred_element_type=F32)

    @pl.when(j == pl.num_programs(1) - 1)
    def _():
        y = o_ref[...]
        yn = y * _rms_scale(y) * gtail_ref[...]
        if tail == "norm_inplace":
            o_ref[...] = yn
        else:
            h_ref[...] = yn.astype(BF16)


def _ffn(x, g_in, wg, wu, wd, g_tail, *, layer, tail, tm, tf):
    m, d = x.shape
    n_chunks = wg.shape[2] // tf
    tile = pl.BlockSpec((tm, d), lambda i, j: (i, 0))
    copy = tail == "norm_copy"
    return pl.pallas_call(
        functools.partial(_ffn_kernel, tail=tail),
        grid=(m // tm, n_chunks),
        in_specs=[
            tile,
            pl.BlockSpec((1, d), lambda i, j: (0, 0)),
            pl.BlockSpec((None, d, tf), lambda i, j: (layer, 0, j)),
            pl.BlockSpec((None, d, tf), lambda i, j: (layer, 0, j)),
            pl.BlockSpec((None, tf, d), lambda i, j: (layer, j, 0)),
            pl.BlockSpec((1, d), lambda i, j: (0, 0)),
        ],
        out_specs=[tile, tile] if copy else tile,
        out_shape=([jax.ShapeDtypeStruct((m, d), F32), jax.ShapeDtypeStruct((m, d), BF16)] if copy
                   else jax.ShapeDtypeStruct((m, d), F32)),
        scratch_shapes=[] if copy else [pltpu.VMEM((tm, d), BF16)],
        compiler_params=_params("parallel", "arbitrary"),
        name="ffn",
    )(x, g_in, wg, wu, wd, g_tail)


def _norm_rope(z, e_ref, gain, rc, rs1, rs2, head_dim):
    tn = z.shape[1]
    zz = (z * z).astype(BF16)
    half = head_dim // 8
    outs = []
    for c in range(tn // V7X_MXU_DIM):
        cols = slice(c * V7X_MXU_DIM, (c + 1) * V7X_MXU_DIM)
        ss = jnp.dot(zz[:, cols], e_ref[...], preferred_element_type=F32)
        y = z[:, cols] * lax.rsqrt(ss * (1.0 / head_dim) + EPS) * gain[:, cols]
        for hh in range(V7X_MXU_DIM // V7X_LANES):
            yc = y[:, hh * V7X_LANES:(hh + 1) * V7X_LANES]
            outs.append(yc * rc + pltpu.roll(yc, V7X_LANES - half, 1) * rs1
                        + pltpu.roll(yc, half, 1) * rs2)
    return jnp.concatenate(outs, axis=1)


def _proj_kernel(*refs, kind, n_tiles, group_dim, head_dim):
    h_ref, w_ref = refs[:2]
    aux, out_ref, z_ref = refs[2:-2], refs[-2], refs[-1]
    t = pl.program_id(1)

    def project(slot):
        z_ref[slot] = jnp.dot(h_ref[...], w_ref[...], preferred_element_type=F32)

    def epilogue(z):
        if kind == "fourier":
            wc_ref, = aux
            u = z.astype(BF16)
            n_groups = z.shape[1] // group_dim
            for g in range(n_groups):
                c = jnp.dot(u[:, g * group_dim:(g + 1) * group_dim], wc_ref[...],
                            preferred_element_type=F32)
                out_ref[:, g * group_dim:(g + 1) * group_dim] = c[:, :group_dim].astype(BF16)
                out_ref[:, (n_groups + g) * group_dim:(n_groups + g + 1) * group_dim] = (
                    c[:, group_dim:].astype(BF16))
        elif kind in ("q", "k"):
            e_ref, gain_ref, rc_ref, rs1_ref, rs2_ref = aux
            y = _norm_rope(z, e_ref, gain_ref[...], rc_ref[...], rs1_ref[...], rs2_ref[...],
                           head_dim)
            if kind == "q":
                out_ref[0] = y.T.astype(BF16)
            else:
                out_ref[...] = y.astype(BF16)
        elif kind == "v":
            out_ref[0] = z.T.astype(BF16)
        else:
            out_ref[...] = jax.nn.sigmoid(z).astype(BF16)

    @pl.when(t == 0)
    def _():
        project(0)

    for parity in range(2):
        @pl.when(jnp.logical_and(jnp.logical_and(t > 0, t < n_tiles), t % 2 == parity))
        def _(parity=parity):
            epilogue(z_ref[1 - parity])
            project(parity)

    @pl.when(t == n_tiles)
    def _():
        epilogue(z_ref[(n_tiles - 1) % 2])


def _project(h, w_in, aux, *, kind, layer, first_block, n_col_blocks, batch, seq, tm, tn, group_dim,
             head_dim):
    m, d = h.shape
    n_tiles, n_s = m // tm, seq // tm
    done = lambda t: jnp.maximum(t - 1, 0)
    const = lambda c, t: (0, 0)
    pos = lambda c, t: (done(t) % n_s, 0)
    aux_specs = {
        "fourier": [pl.BlockSpec(a.shape, const) for a in aux],
        "q": [pl.BlockSpec(a.shape, const) for a in aux[:2]]
             + [pl.BlockSpec((tm, V7X_LANES), pos) for _ in aux[2:]],
        "v": [],
        "gate": [],
    }
    aux_specs["k"] = aux_specs["q"]
    if kind == "fourier":
        out_shape, out_spec = (m, 2 * tn), pl.BlockSpec((tm, 2 * tn), lambda c, t: (done(t), 0))
    elif kind in ("q", "v"):
        out_shape = (batch, tn, seq)
        out_spec = pl.BlockSpec((1, tn, tm), lambda c, t: (done(t) // n_s, 0, done(t) % n_s))
    else:
        out_shape = (m, n_col_blocks * tn)
        out_spec = pl.BlockSpec((tm, tn), lambda c, t: (done(t), c))
    return pl.pallas_call(
        functools.partial(_proj_kernel, kind=kind, n_tiles=n_tiles, group_dim=group_dim,
                          head_dim=head_dim),
        grid=(n_col_blocks, n_tiles + 1),
        in_specs=[
            pl.BlockSpec((tm, d), lambda c, t: (jnp.minimum(t, n_tiles - 1), 0)),
            pl.BlockSpec((None, d, tn), lambda c, t: (layer, 0, first_block + c)),
        ] + aux_specs[kind],
        out_specs=out_spec,
        out_shape=jax.ShapeDtypeStruct(out_shape, BF16),
        scratch_shapes=[pltpu.VMEM((2, tm, tn), F32)],
        compiler_params=_params("arbitrary", "arbitrary"),
        name="proj_" + kind,
    )(h, w_in, *aux)


def _fold_kernel(anti_ref, pick_ref, lo_ref, hi_ref, above_ref, mid_ref, o_ref):
    w = lo_ref.shape[2] // 2
    first = jnp.where(pl.program_id(0) == 0, 1.0, 0.0)
    for b in range(lo_ref.shape[0]):
        rev = (jnp.dot(anti_ref[...], hi_ref[b], preferred_element_type=F32)
               + jnp.dot(pick_ref[...], above_ref[b], preferred_element_type=F32))
        mid = jnp.dot(pick_ref[...], mid_ref[b], preferred_element_type=F32)
        lo = lo_ref[b].astype(F32)
        o_ref[b, :, :w] = (lo[:, :w] + rev[:, :w]).astype(BF16)
        o_ref[b, :, w:] = (lo[:, w:] - rev[:, w:] + first * mid[:, :w]).astype(BF16)


def _fold(ab, *, t):
    batch, seq, two_w = ab.shape
    nb = seq // t
    sub = 16
    r = jnp.arange(t, dtype=jnp.int32)
    anti = (r[:, None] + r[None, :] == t).astype(BF16)
    pick = (r[:, None] + jnp.arange(sub, dtype=jnp.int32)[None, :] == 0).astype(BF16)
    return pl.pallas_call(
        _fold_kernel,
        grid=(nb // 2,),
        in_specs=[
            pl.BlockSpec((t, t), lambda i: (0, 0)),
            pl.BlockSpec((t, sub), lambda i: (0, 0)),
            pl.BlockSpec((batch, t, two_w), lambda i: (0, i, 0)),
            pl.BlockSpec((batch, t, two_w), lambda i: (0, nb - 1 - i, 0)),
            pl.BlockSpec((batch, sub, two_w), lambda i: (0, ((nb - i) * (t // sub)) % (seq // sub), 0)),
            pl.BlockSpec((batch, sub, two_w), lambda i: (0, seq // 2 // sub, 0)),
        ],
        out_specs=pl.BlockSpec((batch, t, two_w), lambda i: (0, i, 0)),
        out_shape=jax.ShapeDtypeStruct((batch, seq // 2, two_w), BF16),
        compiler_params=_params("parallel"),
        name="fold",
    )(anti, pick, ab, ab, ab, ab)


def _fourier_kernel(c_ref, s_ref, a_ref, b_ref, o_ref, acc_ref):
    kk = pl.program_id(1)

    @pl.when(kk == 0)
    def _():
        acc_ref[...] = jnp.zeros_like(acc_ref)

    for b in range(a_ref.shape[0]):
        acc_ref[b] += (jnp.dot(c_ref[...], a_ref[b], preferred_element_type=F32)
                       + jnp.dot(s_ref[...], b_ref[b], preferred_element_type=F32))

    @pl.when(kk == pl.num_programs(1) - 1)
    def _():
        o_ref[...] = acc_ref[...].astype(o_ref.dtype)


def _fourier(cmat, smat, folded, *, tm, tk):
    batch, half_seq, two_w = folded.shape
    seq = cmat.shape[0]
    w = two_w // 2
    return pl.pallas_call(
        _fourier_kernel,
        grid=(seq // tm, half_seq // tk),
        in_specs=[
            pl.BlockSpec((tm, tk), lambda i, kk: (i, kk)),
            pl.BlockSpec((tm, tk), lambda i, kk: (i, kk)),
            pl.BlockSpec((batch, tk, w), lambda i, kk: (0, kk, 0)),
            pl.BlockSpec((batch, tk, w), lambda i, kk: (0, kk, 1)),
        ],
        out_specs=pl.BlockSpec((batch, tm, w), lambda i, kk: (0, i, 0)),
        out_shape=jax.ShapeDtypeStruct((batch, seq, w), BF16),
        scratch_shapes=[pltpu.VMEM((batch, tm, w), F32)],
        compiler_params=_params("parallel", "arbitrary"),
        name="fourier",
    )(cmat, smat, folded, folded)


def _attn_kernel(bounded_ref, qT_ref, k_ref, vT_ref, lq1_ref, lk1_ref, lq2_ref, lk2_ref, sub_ref,
                 o_ref, m_ref, l_ref, acc_ref, *, tk, lam_init):
    qT = qT_ref[0]
    qk_dim = qT.shape[0] // 2
    tq = qT.shape[1]
    row = lax.broadcasted_iota(jnp.int32, qT.shape, 0)
    zero = jnp.zeros_like(qT)
    q2 = jnp.concatenate([jnp.where(row < qk_dim, qT, zero), jnp.where(row >= qk_dim, qT, zero)],
                         axis=1)
    n_kb = k_ref.shape[1] // tk

    l_ref[...] = jnp.zeros_like(l_ref)
    acc_ref[...] = jnp.zeros_like(acc_ref)

    def blocks(j):
        start = pl.multiple_of(j * tk, tk)
        return k_ref[0, pl.ds(start, tk), :], vT_ref[0, :, pl.ds(start, tk)]

    @pl.when(bounded_ref[0] == 1)
    def _():
        def body(j, carry):
            kb, vb = blocks(j)
            s = jnp.dot(kb, q2, preferred_element_type=F32)
            p = jnp.exp2(s)
            l_ref[...] += jnp.sum(p.reshape(tk // 8, 8, 2 * tq), axis=0)
            acc_ref[...] += jnp.dot(vb, p.astype(BF16), preferred_element_type=F32)
            return carry

        lax.fori_loop(0, n_kb, body, 0)

    @pl.when(bounded_ref[0] != 1)
    def _():
        m_ref[...] = jnp.full_like(m_ref, NEG_BIG)

        def body(j, carry):
            kb, vb = blocks(j)
            s = jnp.dot(kb, q2, preferred_element_type=F32)
            m_prev = m_ref[...]
            m_new = jnp.maximum(m_prev, jnp.max(s, axis=0, keepdims=True))
            alpha = jnp.exp2(m_prev - m_new)
            p = jnp.exp2(s - m_new)
            l_ref[...] = alpha * l_ref[...] + jnp.sum(p.reshape(tk // 8, 8, 2 * tq), axis=0)
            acc_ref[...] = acc_ref[...] * alpha + jnp.dot(vb, p.astype(BF16),
                                                          preferred_element_type=F32)
            m_ref[...] = m_new
            return carry

        lax.fori_loop(0, n_kb, body, 0)

    lam = (jnp.exp(jnp.sum(lq1_ref[...] * lk1_ref[...], axis=-1, keepdims=True))
           - jnp.exp(jnp.sum(lq2_ref[...] * lk2_ref[...], axis=-1, keepdims=True)) + lam_init)
    l = jnp.sum(l_ref[...], axis=0, keepdims=True)
    o = acc_ref[:, :tq] * (1.0 / l[:, :tq]) - acc_ref[:, tq:] * (lam / l[:, tq:])
    scale = lax.rsqrt(jnp.mean(o * o, axis=0, keepdims=True) + EPS)
    y = o * scale * (sub_ref[...] * (1.0 - lam_init))
    o_ref[0] = y.T.astype(o_ref.dtype)


def _attention(bounded, qT, k, vT, lq1, lk1, lq2, lk2, sub, *, lam_init, n_heads, tq, tk):
    batch, d_qk, seq = qT.shape
    d_v = vT.shape[1]
    hq, hv = d_qk // n_heads, d_v // n_heads
    vec = lambda b, h, i, flag: (0, 0)
    return pl.pallas_call(
        functools.partial(_attn_kernel, tk=tk, lam_init=lam_init),
        grid_spec=pltpu.PrefetchScalarGridSpec(
            num_scalar_prefetch=1,
            grid=(batch, n_heads, seq // tq),
            in_specs=[
                pl.BlockSpec((1, hq, tq), lambda b, h, i, flag: (b, h, i)),
                pl.BlockSpec((1, seq, hq), lambda b, h, i, flag: (b, 0, h)),
                pl.BlockSpec((1, hv, seq), lambda b, h, i, flag: (b, h, 0)),
                pl.BlockSpec(lq1.shape, vec),
                pl.BlockSpec(lk1.shape, vec),
                pl.BlockSpec(lq2.shape, vec),
                pl.BlockSpec(lk2.shape, vec),
                pl.BlockSpec(sub.shape, vec),
            ],
            out_specs=pl.BlockSpec((1, tq, hv), lambda b, h, i, flag: (b, i, h)),
            scratch_shapes=[
                pltpu.VMEM((1, 2 * tq), F32),
                pltpu.VMEM((8, 2 * tq), F32),
                pltpu.VMEM((hv, 2 * tq), F32),
            ],
        ),
        out_shape=jax.ShapeDtypeStruct((batch, seq, d_v), BF16),
        compiler_params=_params("parallel", "parallel", "arbitrary"),
        name="attn",
    )(bounded, qT, k, vT, lq1, lk1, lq2, lk2, sub)


def _merge_kernel(x_ref, f_ref, o_ref, sg_ref, pf_ref, pa_ref, wo_ref, out_ref):
    d = x_ref.shape[1]
    bf = jnp.dot(f_ref[...], pf_ref[...], preferred_element_type=F32)
    ba = jnp.dot(o_ref[...], pa_ref[...], preferred_element_type=F32)
    mix = sg_ref[:, :d].astype(F32) * bf + sg_ref[:, d:].astype(F32) * ba
    out_ref[...] = x_ref[...] + jnp.dot(mix.astype(BF16), wo_ref[...], preferred_element_type=F32)


def _merge(x, f, o, sg, p_f, p_a, w_o, *, layer, tm):
    m, d = x.shape
    row = lambda i: (i, 0)
    stacked = lambda w: pl.BlockSpec((None,) + w.shape[1:], lambda i: (layer, 0, 0))
    return pl.pallas_call(
        _merge_kernel,
        grid=(m // tm,),
        in_specs=[
            pl.BlockSpec((tm, d), row),
            pl.BlockSpec((tm, f.shape[1]), row),
            pl.BlockSpec((tm, o.shape[1]), row),
            pl.BlockSpec((tm, sg.shape[1]), row),
            stacked(p_f),
            stacked(p_a),
            stacked(w_o),
        ],
        out_specs=pl.BlockSpec((tm, d), row),
        out_shape=jax.ShapeDtypeStruct((m, d), F32),
        compiler_params=_params("parallel"),
        name="merge",
    )(x, f, o, sg, p_f, p_a, w_o)


def _dft_tables(seq, group_dim):
    s = jnp.arange(seq // 2, dtype=jnp.int32)[None, :]
    hi = jnp.arange(seq // V7X_LANES, dtype=jnp.int32)[:, None]
    lo = jnp.arange(V7X_LANES, dtype=jnp.int32)[:, None]
    ang_hi = ((hi * V7X_LANES * s) % seq).astype(F32) * (2.0 * math.pi / seq)
    ang_lo = ((lo * s) % seq).astype(F32) * (2.0 * math.pi / seq)
    ca, sa = jnp.cos(ang_hi)[:, None, :], jnp.sin(ang_hi)[:, None, :]
    cb, sb = jnp.cos(ang_lo)[None, :, :], jnp.sin(ang_lo)[None, :, :]
    k_par = (1 - 2 * (jnp.arange(seq, dtype=jnp.int32) % 2)).astype(F32)[:, None]
    col0 = s == 0
    cmat = jnp.where(col0, 0.5, (ca * cb - sa * sb).reshape(seq, seq // 2)).astype(BF16)
    smat = jnp.where(col0, k_par, (-(sa * cb + ca * sb)).reshape(seq, seq // 2)).astype(BF16)
    c = jnp.arange(group_dim, dtype=jnp.int32)
    angc = ((c[:, None] * c[None, :]) % group_dim).astype(F32) * (2.0 * math.pi / group_dim)
    ortho = 1.0 / math.sqrt(seq * group_dim)
    wc = (jnp.concatenate([jnp.cos(angc), jnp.sin(angc)], axis=1) * ortho).astype(BF16)
    return cmat, smat, wc


def _rope_lane_tables(seq, head_dim):
    rope_dim = head_dim // 4
    half = rope_dim // 2
    pos = jnp.arange(seq, dtype=F32)
    inv_freq = ROPE_THETA ** (-jnp.arange(0, rope_dim, 2, dtype=F32) / rope_dim)
    ang = pos[:, None] * inv_freq[None, :]
    cos, sin = jnp.cos(ang), jnp.sin(ang)
    ones = jnp.ones((seq, head_dim - rope_dim), F32)
    zeros_h = jnp.zeros((seq, half), F32)
    zeros_r = jnp.zeros((seq, head_dim - rope_dim), F32)
    rc = jnp.concatenate([cos, cos, ones], axis=1)
    rs1 = jnp.concatenate([-sin, zeros_h, zeros_r], axis=1)
    rs2 = jnp.concatenate([zeros_h, sin, zeros_r], axis=1)
    reps = V7X_LANES // head_dim
    return tuple(jnp.tile(t, (1, reps)) for t in (rc, rs1, rs2))


def kernel(x, norm_ffa, ffa_gate, ffa_up, ffa_down, norm_mix, w_in, q_norm, k_norm, lambda_q1,
           lambda_k1, lambda_q2, lambda_k2, subln, p_f, p_a, w_o, norm_ffb, ffb_gate, ffb_up,
           ffb_down, norm_out):
    batch, seq, d = x.shape
    depth = w_in.shape[0]
    head_dim = q_norm.shape[1]
    v_dim = subln.shape[1]
    d_f = p_f.shape[1]
    d_v = p_a.shape[1]
    n_heads = d_v // v_dim
    group_dim = d_f // N_FOURIER_GROUPS
    tn = d_f
    assert w_in.shape[2] == 4 * tn + 2 * d and d_v == tn and n_heads * 2 * head_dim == tn
    assert 2 * head_dim == V7X_LANES and v_dim == V7X_LANES

    m = batch * seq
    tm_ffn_a = min(1024, seq)
    tm_ffn_b = min(1024, seq)
    tm_in = min(512, seq)
    tm_merge = min(256, seq)
    t_dft = min(1024, seq)
    t_fold = min(512, seq // 2)
    tq = min(1024, seq)
    tk = min(2048, seq)
    tf = 512

    cmat, smat, wc = _dft_tables(seq, group_dim)
    rc, rs1, rs2 = _rope_lane_tables(seq, head_dim)
    blk = jnp.arange(V7X_MXU_DIM, dtype=jnp.int32) // head_dim
    e = (blk[:, None] == blk[None, :]).astype(BF16)
    q_scale = head_dim ** -0.5 * LOG2E

    ffa_gate, ffa_up, ffa_down, w_in, p_f, p_a, w_o, ffb_gate, ffb_up, ffb_down = (
        w.astype(BF16) for w in (ffa_gate, ffa_up, ffa_down, w_in, p_f, p_a, w_o, ffb_gate, ffb_up,
                                 ffb_down))
    xs = x.reshape(m, d)
    for i in range(depth):
        lam_init = 0.8 - 0.6 * math.exp(-0.3 * i)
        xs, h = _ffn(xs, norm_ffa[i][None], ffa_gate, ffa_up, ffa_down, norm_mix[i][None], layer=i,
                     tail="norm_copy", tm=tm_ffn_a, tf=tf)
        qg = jnp.tile(q_norm[i], tn // head_dim)[None] * q_scale
        kg = jnp.tile(k_norm[i], tn // head_dim)[None]
        project = functools.partial(_project, h, w_in, layer=i, batch=batch, seq=seq, tm=tm_in,
                                    tn=tn, group_dim=group_dim, head_dim=head_dim)
        ab = project((wc,), kind="fourier", first_block=0, n_col_blocks=1)
        qT = project((e, qg, rc, rs1, rs2), kind="q", first_block=1, n_col_blocks=1)
        k = project((e, kg, rc, rs1, rs2), kind="k", first_block=2, n_col_blocks=1)
        vT = project((), kind="v", first_block=3, n_col_blocks=1)
        sg = project((), kind="gate", first_block=4, n_col_blocks=2 * d // tn)
        folded = _fold(ab.reshape(batch, seq, 2 * tn), t=t_fold)
        f = _fourier(cmat, smat, folded, tm=t_dft, tk=min(t_dft, seq // 2))
        score_bound = head_dim * q_scale * jnp.max(jnp.abs(q_norm[i])) * jnp.max(jnp.abs(k_norm[i]))
        bounded = (score_bound <= SCORE_BOUND_LIMIT).astype(jnp.int32).reshape(1)
        o = _attention(bounded, qT, k.reshape(batch, seq, tn), vT, lambda_q1[i][None],
                       lambda_k1[i][None], lambda_q2[i][None], lambda_k2[i][None],
                       subln[i][:, None], lam_init=lam_init, n_heads=n_heads, tq=tq, tk=tk)
        xs = _merge(xs, f.reshape(m, d_f), o.reshape(m, d_v), sg, p_f, p_a, w_o, layer=i,
                    tm=tm_merge)
        xs = _ffn(xs, norm_ffb[i][None], ffb_gate, ffb_up, ffb_down, norm_out[i][None], layer=i,
                  tail="norm_inplace", tm=tm_ffn_b, tf=tf)
    return xs.reshape(batch, seq, d)
```

```python
import functools
import math

import jax
import jax.numpy as jnp
from jax import lax
from jax.experimental import pallas as pl
from jax.experimental.pallas import tpu as pltpu

F32 = jnp.float32
BF16 = jnp.bfloat16

N_FOURIER_GROUPS = 4
ROPE_THETA = 500000.0
EPS = 1e-6
LOG2E = 1.4426950408889634
NEG_BIG = -1e30
SCORE_BOUND_LIMIT = 64.0

V7X_LANES = 128
V7X_MXU_DIM = 256
V7X_VMEM_LIMIT_BYTES = 60 * 1024 * 1024


def _params(*semantics):
    return pltpu.CompilerParams(dimension_semantics=semantics,
                                vmem_limit_bytes=V7X_VMEM_LIMIT_BYTES)


def _rms_scale(x):
    return lax.rsqrt(jnp.mean(x * x, axis=-1, keepdims=True) + EPS)


def _ffn_kernel(x_ref, gin_ref, wg_ref, wu_ref, wd_ref, gtail_ref, o_ref, h_ref, *, tail):
    j = pl.program_id(1)

    def chunk(h):
        g = jnp.dot(h, wg_ref[...], preferred_element_type=F32)
        u = jnp.dot(h, wu_ref[...], preferred_element_type=F32)
        a = (0.5 * g) * jax.nn.sigmoid(g) * u
        return jnp.dot(a.astype(BF16), wd_ref[...], preferred_element_type=F32)

    @pl.when(j == 0)
    def _():
        x = x_ref[...]
        h = (x * _rms_scale(x) * gin_ref[...]).astype(BF16)
        h_ref[...] = h
        o_ref[...] = x + chunk(h)

    @pl.when(j > 0)
    def _():
        o_ref[...] += chunk(h_ref[...])

    @pl.when(j == pl.num_programs(1) - 1)
    def _():
        y = o_ref[...]
        yn = y * _rms_scale(y) * gtail_ref[...]
        if tail == "norm_inplace":
            o_ref[...] = yn
        else:
            h_ref[...] = yn.astype(BF16)


def _ffn(x, g_in, wg, wu, wd, g_tail, *, layer, tail, tm, tf):
    m, d = x.shape
    n_chunks = wg.shape[2] // tf
    tile = pl.BlockSpec((tm, d), lambda i, j: (i, 0))
    copy = tail == "norm_copy"
    return pl.pallas_call(
        functools.partial(_ffn_kernel, tail=tail),
        grid=(m // tm, n_chunks),
        in_specs=[
            tile,
            pl.BlockSpec((1, d), lambda i, j: (0, 0)),
            pl.BlockSpec((None, d, tf), lambda i, j: (layer, 0, j)),
            pl.BlockSpec((None, d, tf), lambda i, j: (layer, 0, j)),
            pl.BlockSpec((None, tf, d), lambda i, j: (layer, j, 0)),
            pl.BlockSpec((1, d), lambda i, j: (0, 0)),
        ],
        out_specs=[tile, tile] if copy else tile,
        out_shape=([jax.ShapeDtypeStruct((m, d), F32), jax.ShapeDtypeStruct((m, d), BF16)] if copy
                   else jax.ShapeDtypeStruct((m, d), F32)),
        scratch_shapes=[] if copy else [pltpu.VMEM((tm, d), BF16)],
        compiler_params=_params("parallel", "arbitrary"),
        name="ffn",
    )(x, g_in, wg, wu, wd, g_tail)


def _norm_rope(z, e_ref, gain, rc, rs1, rs2, head_dim):
    tn = z.shape[1]
    zz = (z * z).astype(BF16)
    half = head_dim // 8
    outs = []
    for c in range(tn // V7X_MXU_DIM):
        cols = slice(c * V7X_MXU_DIM, (c + 1) * V7X_MXU_DIM)
        ss = jnp.dot(zz[:, cols], e_ref[...], preferred_element_type=F32)
        y = z[:, cols] * lax.rsqrt(ss * (1.0 / head_dim) + EPS) * gain[:, cols]
        for hh in range(V7X_MXU_DIM // V7X_LANES):
            yc = y[:, hh * V7X_LANES:(hh + 1) * V7X_LANES]
            outs.append(yc * rc + pltpu.roll(yc, V7X_LANES - half, 1) * rs1
                        + pltpu.roll(yc, half, 1) * rs2)
    return jnp.concatenate(outs, axis=1)


def _proj_kernel(*refs, kind, n_tiles, group_dim, head_dim):
    h_ref, w_ref = refs[:2]
    aux, out_ref, z_ref = refs[2:-2], refs[-2], refs[-1]
    t = pl.program_id(1)

    def project(slot):
        z_ref[slot] = jnp.dot(h_ref[...], w_ref[...], preferred_element_type=F32)

    def epilogue(z):
        if kind == "fourier":
            wc_ref, = aux
            u = z.astype(BF16)
            n_groups = z.shape[1] // group_dim
            for g in range(n_groups):
                c = jnp.dot(u[:, g * group_dim:(g + 1) * group_dim], wc_ref[...],
                            preferred_element_type=F32)
                out_ref[:, g * group_dim:(g + 1) * group_dim] = c[:, :group_dim].astype(BF16)
                out_ref[:, (n_groups + g) * group_dim:(n_groups + g + 1) * group_dim] = (
                    c[:, group_dim:].astype(BF16))
        elif kind in ("q", "k"):
            e_ref, gain_ref, rc_ref, rs1_ref, rs2_ref = aux
            y = _norm_rope(z, e_ref, gain_ref[...], rc_ref[...], rs1_ref[...], rs2_ref[...],
                           head_dim)
            if kind == "q":
                out_ref[0] = y.T.astype(BF16)
            else:
                out_ref[...] = y.astype(BF16)
        elif kind == "v":
            out_ref[0] = z.T.astype(BF16)
        else:
            out_ref[...] = jax.nn.sigmoid(z).astype(BF16)

    @pl.when(t == 0)
    def _():
        project(0)

    for parity in range(2):
        @pl.when(jnp.logical_and(jnp.logical_and(t > 0, t < n_tiles), t % 2 == parity))
        def _(parity=parity):
            epilogue(z_ref[1 - parity])
            project(parity)

    @pl.when(t == n_tiles)
    def _():
        epilogue(z_ref[(n_tiles - 1) % 2])


def _project(h, w_in, aux, *, kind, layer, first_block, n_col_blocks, batch, seq, tm, tn, group_dim,
             head_dim):
    m, d = h.shape
    n_tiles, n_s = m // tm, seq // tm
    done = lambda t: jnp.maximum(t - 1, 0)
    const = lambda c, t: (0, 0)
    pos = lambda c, t: (done(t) % n_s, 0)
    aux_specs = {
        "fourier": [pl.BlockSpec(a.shape, const) for a in aux],
        "q": [pl.BlockSpec(a.shape, const) for a in aux[:2]]
             + [pl.BlockSpec((tm, V7X_LANES), pos) for _ in aux[2:]],
        "v": [],
        "gate": [],
    }
    aux_specs["k"] = aux_specs["q"]
    if kind == "fourier":
        out_shape, out_spec = (m, 2 * tn), pl.BlockSpec((tm, 2 * tn), lambda c, t: (done(t), 0))
    elif kind in ("q", "v"):
        out_shape = (batch, tn, seq)
        out_spec = pl.BlockSpec((1, tn, tm), lambda c, t: (done(t) // n_s, 0, done(t) % n_s))
    else:
        out_shape = (m, n_col_blocks * tn)
        out_spec = pl.BlockSpec((tm, tn), lambda c, t: (done(t), c))
    return pl.pallas_call(
        functools.partial(_proj_kernel, kind=kind, n_tiles=n_tiles, group_dim=group_dim,
                          head_dim=head_dim),
        grid=(n_col_blocks, n_tiles + 1),
        in_specs=[
            pl.BlockSpec((tm, d), lambda c, t: (jnp.minimum(t, n_tiles - 1), 0)),
            pl.BlockSpec((None, d, tn), lambda c, t: (layer, 0, first_block + c)),
        ] + aux_specs[kind],
        out_specs=out_spec,
        out_shape=jax.ShapeDtypeStruct(out_shape, BF16),
        scratch_shapes=[pltpu.VMEM((2, tm, tn), F32)],
        compiler_params=_params("arbitrary", "arbitrary"),
        name="proj_" + kind,
    )(h, w_in, *aux)


def _fold_kernel(anti_ref, pick_ref, lo_ref, hi_ref, above_ref, mid_ref, o_ref):
    w = lo_ref.shape[2] // 2
    first = jnp.where(pl.program_id(0) == 0, 1.0, 0.0)
    for b in range(lo_ref.shape[0]):
        rev = (jnp.dot(anti_ref[...], hi_ref[b], preferred_element_type=F32)
               + jnp.dot(pick_ref[...], above_ref[b], preferred_element_type=F32))
        mid = jnp.dot(pick_ref[...], mid_ref[b], preferred_element_type=F32)
        lo = lo_ref[b].astype(F32)
        o_ref[b, :, :w] = (lo[:, :w] + rev[:, :w]).astype(BF16)
        o_ref[b, :, w:] = (lo[:, w:] - rev[:, w:] + first * mid[:, :w]).astype(BF16)


def _fold(ab, *, t):
    batch, seq, two_w = ab.shape
    nb = seq // t
    sub = 16
    r = jnp.arange(t, dtype=jnp.int32)
    anti = (r[:, None] + r[None, :] == t).astype(BF16)
    pick = (r[:, None] + jnp.arange(sub, dtype=jnp.int32)[None, :] == 0).astype(BF16)
    return pl.pallas_call(
        _fold_kernel,
        grid=(nb // 2,),
        in_specs=[
            pl.BlockSpec((t, t), lambda i: (0, 0)),
            pl.BlockSpec((t, sub), lambda i: (0, 0)),
            pl.BlockSpec((batch, t, two_w), lambda i: (0, i, 0)),
            pl.BlockSpec((batch, t, two_w), lambda i: (0, nb - 1 - i, 0)),
            pl.BlockSpec((batch, sub, two_w), lambda i: (0, ((nb - i) * (t // sub)) % (seq // sub), 0)),
            pl.BlockSpec((batch, sub, two_w), lambda i: (0, seq // 2 // sub, 0)),
        ],
        out_specs=pl.BlockSpec((batch, t, two_w), lambda i: (0, i, 0)),
        out_shape=jax.ShapeDtypeStruct((batch, seq // 2, two_w), BF16),
        compiler_params=_params("parallel"),
        name="fold",
    )(anti, pick, ab, ab, ab, ab)


def _fourier_kernel(c_ref, s_ref, a_ref, b_ref, o_ref, acc_ref):
    kk = pl.program_id(1)

    @pl.when(kk == 0)
    def _():
        acc_ref[...] = jnp.zeros_like(acc_ref)

    for b in range(a_ref.shape[0]):
        acc_ref[b] += (jnp.dot(c_ref[...], a_ref[b], preferred_element_type=F32)
                       + jnp.dot(s_ref[...], b_ref[b], preferred_element_type=F32))

    @pl.when(kk == pl.num_programs(1) - 1)
    def _():
        o_ref[...] = acc_ref[...].astype(o_ref.dtype)


def _fourier(cmat, smat, folded, *, tm, tk):
    batch, half_seq, two_w = folded.shape
    seq = cmat.shape[0]
    w = two_w // 2
    return pl.pallas_call(
        _fourier_kernel,
        grid=(seq // tm, half_seq // tk),
        in_specs=[
            pl.BlockSpec((tm, tk), lambda i, kk: (i, kk)),
            pl.BlockSpec((tm, tk), lambda i, kk: (i, kk)),
            pl.BlockSpec((batch, tk, w), lambda i, kk: (0, kk, 0)),
            pl.BlockSpec((batch, tk, w), lambda i, kk: (0, kk, 1)),
        ],
        out_specs=pl.BlockSpec((batch, tm, w), lambda i, kk: (0, i, 0)),
        out_shape=jax.ShapeDtypeStruct((batch, seq, w), BF16),
        scratch_shapes=[pltpu.VMEM((batch, tm, w), F32)],
        compiler_params=_params("parallel", "arbitrary"),
        name="fourier",
    )(cmat, smat, folded, folded)


def _attn_kernel(bounded_ref, qT_ref, k_ref, vT_ref, lq1_ref, lk1_ref, lq2_ref, lk2_ref, sub_ref,
                 o_ref, m_ref, l_ref, acc_ref, *, tk, lam_init):
    qT = qT_ref[0]
    qk_dim = qT.shape[0] // 2
    tq = qT.shape[1]
    row = lax.broadcasted_iota(jnp.int32, qT.shape, 0)
    zero = jnp.zeros_like(qT)
    q2 = jnp.concatenate([jnp.where(row < qk_dim, qT, zero), jnp.where(row >= qk_dim, qT, zero)],
                         axis=1)
    n_kb = k_ref.shape[1] // tk

    l_ref[...] = jnp.zeros_like(l_ref)
    acc_ref[...] = jnp.zeros_like(acc_ref)

    def blocks(j):
        start = pl.multiple_of(j * tk, tk)
        return k_ref[0, pl.ds(start, tk), :], vT_ref[0, :, pl.ds(start, tk)]

    @pl.when(bounded_ref[0] == 1)
    def _():
        def body(j, carry):
            kb, vb = blocks(j)
            s = jnp.dot(kb, q2, preferred_element_type=F32)
            p = jnp.exp2(s)
            l_ref[...] += jnp.sum(p.reshape(tk // 8, 8, 2 * tq), axis=0)
            acc_ref[...] += jnp.dot(vb, p.astype(BF16), preferred_element_type=F32)
            return carry

        lax.fori_loop(0, n_kb, body, 0)

    @pl.when(bounded_ref[0] != 1)
    def _():
        m_ref[...] = jnp.full_like(m_ref, NEG_BIG)

        def body(j, carry):
            kb, vb = blocks(j)
            s = jnp.dot(kb, q2, preferred_element_type=F32)
            m_prev = m_ref[...]
            m_new = jnp.maximum(m_prev, jnp.max(s, axis=0, keepdims=True))
            alpha = jnp.exp2(m_prev - m_new)
            p = jnp.exp2(s - m_new)
            l_ref[...] = alpha * l_ref[...] + jnp.sum(p.reshape(tk // 8, 8, 2 * tq), axis=0)
            acc_ref[...] = acc_ref[...] * alpha + jnp.dot(vb, p.astype(BF16),
                                                          preferred_element_type=F32)
            m_ref[...] = m_new
            return carry

        lax.fori_loop(0, n_kb, body, 0)

    lam = (jnp.exp(jnp.sum(lq1_ref[...] * lk1_ref[...], axis=-1, keepdims=True))
           - jnp.exp(jnp.sum(lq2_ref[...] * lk2_ref[...], axis=-1, keepdims=True)) + lam_init)
    l = jnp.sum(l_ref[...], axis=0, keepdims=True)
    o = acc_ref[:, :tq] * (1.0 / l[:, :tq]) - acc_ref[:, tq:] * (lam / l[:, tq:])
    scale = lax.rsqrt(jnp.mean(o * o, axis=0, keepdims=True) + EPS)
    y = o * scale * (sub_ref[...] * (1.0 - lam_init))
    o_ref[0] = y.T.astype(o_ref.dtype)


def _attention(bounded, qT, k, vT, lq1, lk1, lq2, lk2, sub, *, lam_init, n_heads, tq, tk):
    batch, d_qk, seq = qT.shape
    d_v = vT.shape[1]
    hq, hv = d_qk // n_heads, d_v // n_heads
    vec = lambda b, h, i, flag: (0, 0)
    return pl.pallas_call(
        functools.partial(_attn_kernel, tk=tk, lam_init=lam_init),
        grid_spec=pltpu.PrefetchScalarGridSpec(
            num_scalar_prefetch=1,
            grid=(batch, n_heads, seq // tq),
            in_specs=[
                pl.BlockSpec((1, hq, tq), lambda b, h, i, flag: (b, h, i)),
                pl.BlockSpec((1, seq, hq), lambda b, h, i, flag: (b, 0, h)),
                pl.BlockSpec((1, hv, seq), lambda b, h, i, flag: (b, h, 0)),
                pl.BlockSpec(lq1.shape, vec),
                pl.BlockSpec(lk1.shape, vec),
                pl.BlockSpec(lq2.shape, vec),
                pl.BlockSpec(lk2.shape, vec),
                pl.BlockSpec(sub.shape, vec),
            ],
            out_specs=pl.BlockSpec((1, tq, hv), lambda b, h, i, flag: (b, i, h)),
            scratch_shapes=[
                pltpu.VMEM((1, 2 * tq), F32),
                pltpu.VMEM((8, 2 * tq), F32),
                pltpu.VMEM((hv, 2 * tq), F32),
            ],
        ),
        out_shape=jax.ShapeDtypeStruct((batch, seq, d_v), BF16),
        compiler_params=_params("parallel", "parallel", "arbitrary"),
        name="attn",
    )(bounded, qT, k, vT, lq1, lk1, lq2, lk2, sub)


def _merge_kernel(x_ref, f_ref, o_ref, sg_ref, pf_ref, pa_ref, wo_ref, out_ref):
    d = x_ref.shape[1]
    bf = jnp.dot(f_ref[...], pf_ref[...], preferred_element_type=F32)
    ba = jnp.dot(o_ref[...], pa_ref[...], preferred_element_type=F32)
    mix = sg_ref[:, :d].astype(F32) * bf + sg_ref[:, d:].astype(F32) * ba
    out_ref[...] = x_ref[...] + jnp.dot(mix.astype(BF16), wo_ref[...], preferred_element_type=F32)


def _merge(x, f, o, sg, p_f, p_a, w_o, *, layer, tm):
    m, d = x.shape
    row = lambda i: (i, 0)
    stacked = lambda w: pl.BlockSpec((None,) + w.shape[1:], lambda i: (layer, 0, 0))
    return pl.pallas_call(
        _merge_kernel,
        grid=(m // tm,),
        in_specs=[
            pl.BlockSpec((tm, d), row),
            pl.BlockSpec((tm, f.shape[1]), row),
            pl.BlockSpec((tm, o.shape[1]), row),
            pl.BlockSpec((tm, sg.shape[1]), row),
            stacked(p_f),
            stacked(p_a),
            stacked(w_o),
        ],
        out_specs=pl.BlockSpec((tm, d), row),
        out_shape=jax.ShapeDtypeStruct((m, d), F32),
        compiler_params=_params("parallel"),
        name="merge",
    )(x, f, o, sg, p_f, p_a, w_o)


def _dft_tables(seq, group_dim):
    s = jnp.arange(seq // 2, dtype=jnp.int32)[None, :]
    hi = jnp.arange(seq // V7X_LANES, dtype=jnp.int32)[:, None]
    lo = jnp.arange(V7X_LANES, dtype=jnp.int32)[:, None]
    ang_hi = ((hi * V7X_LANES * s) % seq).astype(F32) * (2.0 * math.pi / seq)
    ang_lo = ((lo * s) % seq).astype(F32) * (2.0 * math.pi / seq)
    ca, sa = jnp.cos(ang_hi)[:, None, :], jnp.sin(ang_hi)[:, None, :]
    cb, sb = jnp.cos(ang_lo)[None, :, :], jnp.sin(ang_lo)[None, :, :]
    k_par = (1 - 2 * (jnp.arange(seq, dtype=jnp.int32) % 2)).astype(F32)[:, None]
    col0 = s == 0
    cmat = jnp.where(col0, 0.5, (ca * cb - sa * sb).reshape(seq, seq // 2)).astype(BF16)
    smat = jnp.where(col0, k_par, (-(sa * cb + ca * sb)).reshape(seq, seq // 2)).astype(BF16)
    c = jnp.arange(group_dim, dtype=jnp.int32)
    angc = ((c[:, None] * c[None, :]) % group_dim).astype(F32) * (2.0 * math.pi / group_dim)
    ortho = 1.0 / math.sqrt(seq * group_dim)
    wc = (jnp.concatenate([jnp.cos(angc), jnp.sin(angc)], axis=1) * ortho).astype(BF16)
    return cmat, smat, wc


def _rope_lane_tables(seq, head_dim):
    rope_dim = head_dim // 4
    half = rope_dim // 2
    pos = jnp.arange(seq, dtype=F32)
    inv_freq = ROPE_THETA ** (-jnp.arange(0, rope_dim, 2, dtype=F32) / rope_dim)
    ang = pos[:, None] * inv_freq[None, :]
    cos, sin = jnp.cos(ang), jnp.sin(ang)
    ones = jnp.ones((seq, head_dim - rope_dim), F32)
    zeros_h = jnp.zeros((seq, half), F32)
    zeros_r = jnp.zeros((seq, head_dim - rope_dim), F32)
    rc = jnp.concatenate([cos, cos, ones], axis=1)
    rs1 = jnp.concatenate([-sin, zeros_h, zeros_r], axis=1)
    rs2 = jnp.concatenate([zeros_h, sin, zeros_r], axis=1)
    reps = V7X_LANES // head_dim
    return tuple(jnp.tile(t, (1, reps)) for t in (rc, rs1, rs2))


def kernel(x, norm_ffa, ffa_gate, ffa_up, ffa_down, norm_mix, w_in, q_norm, k_norm, lambda_q1,
           lambda_k1, lambda_q2, lambda_k2, subln, p_f, p_a, w_o, norm_ffb, ffb_gate, ffb_up,
           ffb_down, norm_out):
    batch, seq, d = x.shape
    depth = w_in.shape[0]
    head_dim = q_norm.shape[1]
    v_dim = subln.shape[1]
    d_f = p_f.shape[1]
    d_v = p_a.shape[1]
    n_heads = d_v // v_dim
    group_dim = d_f // N_FOURIER_GROUPS
    tn = d_f
    assert w_in.shape[2] == 4 * tn + 2 * d and d_v == tn and n_heads * 2 * head_dim == tn
    assert 2 * head_dim == V7X_LANES and v_dim == V7X_LANES

    m = batch * seq
    tm_ffn_a = min(1024, seq)
    tm_ffn_b = min(1024, seq)
    tm_in = min(512, seq)
    tm_merge = min(256, seq)
    t_dft = min(1024, seq)
    t_fold = min(512, seq // 2)
    tq = min(2048, seq)
    tk = min(2048, seq)
    tf = 512

    cmat, smat, wc = _dft_tables(seq, group_dim)
    rc, rs1, rs2 = _rope_lane_tables(seq, head_dim)
    blk = jnp.arange(V7X_MXU_DIM, dtype=jnp.int32) // head_dim
    e = (blk[:, None] == blk[None, :]).astype(BF16)
    q_scale = head_dim ** -0.5 * LOG2E

    ffa_gate, ffa_up, ffa_down, w_in, p_f, p_a, w_o, ffb_gate, ffb_up, ffb_down = (
        w.astype(BF16) for w in (ffa_gate, ffa_up, ffa_down, w_in, p_f, p_a, w_o, ffb_gate, ffb_up,
                                 ffb_down))
    xs = x.reshape(m, d)
    for i in range(depth):
        lam_init = 0.8 - 0.6 * math.exp(-0.3 * i)
        xs, h = _ffn(xs, norm_ffa[i][None], ffa_gate, ffa_up, ffa_down, norm_mix[i][None], layer=i,
                     tail="norm_copy", tm=tm_ffn_a, tf=tf)
        qg = jnp.tile(q_norm[i], tn // head_dim)[None] * q_scale
        kg = jnp.tile(k_norm[i], tn // head_dim)[None]
        project = functools.partial(_project, h, w_in, layer=i, batch=batch, seq=seq, tm=tm_in,
                                    tn=tn, group_dim=group_dim, head_dim=head_dim)
        ab = project((wc,), kind="fourier", first_block=0, n_col_blocks=1)
        qT = project((e, qg, rc, rs1, rs2), kind="q", first_block=1, n_col_blocks=1)
        k = project((e, kg, rc, rs1, rs2), kind="k", first_block=2, n_col_blocks=1)
        vT = project((), kind="v", first_block=3, n_col_blocks=1)
        sg = project((), kind="gate", first_block=4, n_col_blocks=2 * d // tn)
        folded = _fold(ab.reshape(batch, seq, 2 * tn), t=t_fold)
        f = _fourier(cmat, smat, folded, tm=t_dft, tk=min(t_dft, seq // 2))
        score_bound = head_dim * q_scale * jnp.max(jnp.abs(q_norm[i])) * jnp.max(jnp.abs(k_norm[i]))
        bounded = (score_bound <= SCORE_BOUND_LIMIT).astype(jnp.int32).reshape(1)
        o = _attention(bounded, qT, k.reshape(batch, seq, tn), vT, lambda_q1[i][None],
                       lambda_k1[i][None], lambda_q2[i][None], lambda_k2[i][None],
                       subln[i][:, None], lam_init=lam_init, n_heads=n_heads, tq=tq, tk=tk)
        xs = _merge(xs, f.reshape(m, d_f), o.reshape(m, d_v), sg, p_f, p_a, w_o, layer=i,
                    tm=tm_merge)
        xs = _ffn(xs, norm_ffb[i][None], ffb_gate, ffb_up, ffb_down, norm_out[i][None], layer=i,
                  tail="norm_inplace", tm=tm_ffn_b, tf=tf)
    return xs.reshape(batch, seq, d)
```

```python
import functools
import math

import jax
import jax.numpy as jnp
from jax import lax
from jax.experimental import pallas as pl
from jax.experimental.pallas import tpu as pltpu

F32 = jnp.float32
BF16 = jnp.bfloat16

N_FOURIER_GROUPS = 4
ROPE_THETA = 500000.0
EPS = 1e-6
LOG2E = 1.4426950408889634
NEG_BIG = -1e30
SCORE_BOUND_LIMIT = 64.0

V7X_LANES = 128
V7X_MXU_DIM = 256
V7X_VMEM_LIMIT_BYTES = 60 * 1024 * 1024


def _params(*semantics):
    return pltpu.CompilerParams(dimension_semantics=semantics,
                                vmem_limit_bytes=V7X_VMEM_LIMIT_BYTES)


def _rms_scale(x):
    return lax.rsqrt(jnp.mean(x * x, axis=-1, keepdims=True) + EPS)


def _ffn_kernel(x_ref, gin_ref, wg_ref, wu_ref, wd_ref, gtail_ref, o_ref, h_ref, *, tail):
    j = pl.program_id(1)

    def chunk(h):
        g = jnp.dot(h, wg_ref[...], preferred_element_type=F32)
        u = jnp.dot(h, wu_ref[...], preferred_element_type=F32)
        a = (0.5 * g) * jax.nn.sigmoid(g) * u
        return jnp.dot(a.astype(BF16), wd_ref[...], preferred_element_type=F32)

    @pl.when(j == 0)
    def _():
        x = x_ref[...]
        h = (x * _rms_scale(x) * gin_ref[...]).astype(BF16)
        h_ref[...] = h
        o_ref[...] = x + chunk(h)

    @pl.when(j > 0)
    def _():
        o_ref[...] += chunk(h_ref[...])

    @pl.when(j == pl.num_programs(1) - 1)
    def _():
        y = o_ref[...]
        yn = y * _rms_scale(y) * gtail_ref[...]
        if tail == "norm_inplace":
            o_ref[...] = yn
        else:
            h_ref[...] = yn.astype(BF16)


def _ffn(x, g_in, wg, wu, wd, g_tail, *, layer, tail, tm, tf):
    m, d = x.shape
    n_chunks = wg.shape[2] // tf
    tile = pl.BlockSpec((tm, d), lambda i, j: (i, 0))
    copy = tail == "norm_copy"
    return pl.pallas_call(
        functools.partial(_ffn_kernel, tail=tail),
        grid=(m // tm, n_chunks),
        in_specs=[
            tile,
            pl.BlockSpec((1, d), lambda i, j: (0, 0)),
            pl.BlockSpec((None, d, tf), lambda i, j: (layer, 0, j)),
            pl.BlockSpec((None, d, tf), lambda i, j: (layer, 0, j)),
            pl.BlockSpec((None, tf, d), lambda i, j: (layer, j, 0)),
            pl.BlockSpec((1, d), lambda i, j: (0, 0)),
        ],
        out_specs=[tile, tile] if copy else tile,
        out_shape=([jax.ShapeDtypeStruct((m, d), F32), jax.ShapeDtypeStruct((m, d), BF16)] if copy
                   else jax.ShapeDtypeStruct((m, d), F32)),
        scratch_shapes=[] if copy else [pltpu.VMEM((tm, d), BF16)],
        compiler_params=_params("parallel", "arbitrary"),
        name="ffn",
    )(x, g_in, wg, wu, wd, g_tail)


def _norm_rope(z, e_ref, gain, rc, rs1, rs2, head_dim):
    tn = z.shape[1]
    zz = (z * z).astype(BF16)
    half = head_dim // 8
    outs = []
    for c in range(tn // V7X_MXU_DIM):
        cols = slice(c * V7X_MXU_DIM, (c + 1) * V7X_MXU_DIM)
        ss = jnp.dot(zz[:, cols], e_ref[...], preferred_element_type=F32)
        y = z[:, cols] * lax.rsqrt(ss * (1.0 / head_dim) + EPS) * gain[:, cols]
        for hh in range(V7X_MXU_DIM // V7X_LANES):
            yc = y[:, hh * V7X_LANES:(hh + 1) * V7X_LANES]
            outs.append(yc * rc + pltpu.roll(yc, V7X_LANES - half, 1) * rs1
                        + pltpu.roll(yc, half, 1) * rs2)
    return jnp.concatenate(outs, axis=1)


def _proj_kernel(*refs, kind, n_tiles, group_dim, head_dim):
    h_ref, w_ref = refs[:2]
    aux, out_ref, z_ref = refs[2:-2], refs[-2], refs[-1]
    t = pl.program_id(1)

    def project(slot):
        z_ref[slot] = jnp.dot(h_ref[...], w_ref[...], preferred_element_type=F32)

    def epilogue(z):
        if kind == "fourier":
            wc_ref, = aux
            u = z.astype(BF16)
            n_groups = z.shape[1] // group_dim
            for g in range(n_groups):
                c = jnp.dot(u[:, g * group_dim:(g + 1) * group_dim], wc_ref[...],
                            preferred_element_type=F32)
                out_ref[:, g * group_dim:(g + 1) * group_dim] = c[:, :group_dim].astype(BF16)
                out_ref[:, (n_groups + g) * group_dim:(n_groups + g + 1) * group_dim] = (
                    c[:, group_dim:].astype(BF16))
        elif kind in ("q", "k"):
            e_ref, gain_ref, rc_ref, rs1_ref, rs2_ref = aux
            y = _norm_rope(z, e_ref, gain_ref[...], rc_ref[...], rs1_ref[...], rs2_ref[...],
                           head_dim)
            if kind == "q":
                out_ref[0] = y.T.astype(BF16)
            else:
                out_ref[...] = y.astype(BF16)
        elif kind == "v":
            out_ref[0] = z.T.astype(BF16)
        else:
            out_ref[...] = (0.5 * jnp.tanh(0.5 * z) + 0.5).astype(BF16)

    @pl.when(t == 0)
    def _():
        project(0)

    for parity in range(2):
        @pl.when(jnp.logical_and(jnp.logical_and(t > 0, t < n_tiles), t % 2 == parity))
        def _(parity=parity):
            epilogue(z_ref[1 - parity])
            project(parity)

    @pl.when(t == n_tiles)
    def _():
        epilogue(z_ref[(n_tiles - 1) % 2])


def _project(h, w_in, aux, *, kind, layer, first_block, n_col_blocks, batch, seq, tm, tn, group_dim,
             head_dim):
    m, d = h.shape
    n_tiles, n_s = m // tm, seq // tm
    done = lambda t: jnp.maximum(t - 1, 0)
    const = lambda c, t: (0, 0)
    pos = lambda c, t: (done(t) % n_s, 0)
    aux_specs = {
        "fourier": [pl.BlockSpec(a.shape, const) for a in aux],
        "q": [pl.BlockSpec(a.shape, const) for a in aux[:2]]
             + [pl.BlockSpec((tm, V7X_LANES), pos) for _ in aux[2:]],
        "v": [],
        "gate": [],
    }
    aux_specs["k"] = aux_specs["q"]
    if kind == "fourier":
        out_shape, out_spec = (m, 2 * tn), pl.BlockSpec((tm, 2 * tn), lambda c, t: (done(t), 0))
    elif kind in ("q", "v"):
        out_shape = (batch, tn, seq)
        out_spec = pl.BlockSpec((1, tn, tm), lambda c, t: (done(t) // n_s, 0, done(t) % n_s))
    else:
        out_shape = (m, n_col_blocks * tn)
        out_spec = pl.BlockSpec((tm, tn), lambda c, t: (done(t), c))
    return pl.pallas_call(
        functools.partial(_proj_kernel, kind=kind, n_tiles=n_tiles, group_dim=group_dim,
                          head_dim=head_dim),
        grid=(n_col_blocks, n_tiles + 1),
        in_specs=[
            pl.BlockSpec((tm, d), lambda c, t: (jnp.minimum(t, n_tiles - 1), 0)),
            pl.BlockSpec((None, d, tn), lambda c, t: (layer, 0, first_block + c)),
        ] + aux_specs[kind],
        out_specs=out_spec,
        out_shape=jax.ShapeDtypeStruct(out_shape, BF16),
        scratch_shapes=[pltpu.VMEM((2, tm, tn), F32)],
        compiler_params=_params("arbitrary", "arbitrary"),
        name="proj_" + kind,
    )(h, w_in, *aux)


def _fold_kernel(anti_ref, pick_ref, lo_ref, hi_ref, above_ref, mid_ref, o_ref):
    w = lo_ref.shape[2] // 2
    first = jnp.where(pl.program_id(0) == 0, 1.0, 0.0)
    for b in range(lo_ref.shape[0]):
        rev = (jnp.dot(anti_ref[...], hi_ref[b], preferred_element_type=F32)
               + jnp.dot(pick_ref[...], above_ref[b], preferred_element_type=F32))
        mid = jnp.dot(pick_ref[...], mid_ref[b], preferred_element_type=F32)
        lo = lo_ref[b].astype(F32)
        o_ref[b, :, :w] = (lo[:, :w] + rev[:, :w]).astype(BF16)
        o_ref[b, :, w:] = (lo[:, w:] - rev[:, w:] + first * mid[:, :w]).astype(BF16)


def _fold(ab, *, t):
    batch, seq, two_w = ab.shape
    nb = seq // t
    sub = 16
    r = jnp.arange(t, dtype=jnp.int32)
    anti = (r[:, None] + r[None, :] == t).astype(BF16)
    pick = (r[:, None] + jnp.arange(sub, dtype=jnp.int32)[None, :] == 0).astype(BF16)
    return pl.pallas_call(
        _fold_kernel,
        grid=(nb // 2,),
        in_specs=[
            pl.BlockSpec((t, t), lambda i: (0, 0)),
            pl.BlockSpec((t, sub), lambda i: (0, 0)),
            pl.BlockSpec((batch, t, two_w), lambda i: (0, i, 0)),
            pl.BlockSpec((batch, t, two_w), lambda i: (0, nb - 1 - i, 0)),
            pl.BlockSpec((batch, sub, two_w), lambda i: (0, ((nb - i) * (t // sub)) % (seq // sub), 0)),
            pl.BlockSpec((batch, sub, two_w), lambda i: (0, seq // 2 // sub, 0)),
        ],
        out_specs=pl.BlockSpec((batch, t, two_w), lambda i: (0, i, 0)),
        out_shape=jax.ShapeDtypeStruct((batch, seq // 2, two_w), BF16),
        compiler_params=_params("parallel"),
        name="fold",
    )(anti, pick, ab, ab, ab, ab)


def _fourier_kernel(c_ref, s_ref, a_ref, b_ref, o_ref, acc_ref):
    kk = pl.program_id(1)

    @pl.when(kk == 0)
    def _():
        acc_ref[...] = jnp.zeros_like(acc_ref)

    for b in range(a_ref.shape[0]):
        acc_ref[b] += (jnp.dot(c_ref[...], a_ref[b], preferred_element_type=F32)
                       + jnp.dot(s_ref[...], b_ref[b], preferred_element_type=F32))

    @pl.when(kk == pl.num_programs(1) - 1)
    def _():
        o_ref[...] = acc_ref[...].astype(o_ref.dtype)


def _fourier(cmat, smat, folded, *, tm, tk):
    batch, half_seq, two_w = folded.shape
    seq = cmat.shape[0]
    w = two_w // 2
    return pl.pallas_call(
        _fourier_kernel,
        grid=(seq // tm, half_seq // tk),
        in_specs=[
            pl.BlockSpec((tm, tk), lambda i, kk: (i, kk)),
            pl.BlockSpec((tm, tk), lambda i, kk: (i, kk)),
            pl.BlockSpec((batch, tk, w), lambda i, kk: (0, kk, 0)),
            pl.BlockSpec((batch, tk, w), lambda i, kk: (0, kk, 1)),
        ],
        out_specs=pl.BlockSpec((batch, tm, w), lambda i, kk: (0, i, 0)),
        out_shape=jax.ShapeDtypeStruct((batch, seq, w), BF16),
        scratch_shapes=[pltpu.VMEM((batch, tm, w), F32)],
        compiler_params=_params("parallel", "arbitrary"),
        name="fourier",
    )(cmat, smat, folded, folded)


def _attn_kernel(bounded_ref, qT_ref, k_ref, vT_ref, lq1_ref, lk1_ref, lq2_ref, lk2_ref, sub_ref,
                 o_ref, m_ref, l_ref, acc_ref, *, tk, lam_init):
    qT = qT_ref[0]
    qk_dim = qT.shape[0] // 2
    tq = qT.shape[1]
    row = lax.broadcasted_iota(jnp.int32, qT.shape, 0)
    zero = jnp.zeros_like(qT)
    q2 = jnp.concatenate([jnp.where(row < qk_dim, qT, zero), jnp.where(row >= qk_dim, qT, zero)],
                         axis=1)
    n_kb = k_ref.shape[1] // tk

    l_ref[...] = jnp.zeros_like(l_ref)
    acc_ref[...] = jnp.zeros_like(acc_ref)

    def blocks(j):
        start = pl.multiple_of(j * tk, tk)
        return k_ref[0, pl.ds(start, tk), :], vT_ref[0, :, pl.ds(start, tk)]

    @pl.when(bounded_ref[0] == 1)
    def _():
        def body(j, carry):
            kb, vb = blocks(j)
            s = jnp.dot(kb, q2, preferred_element_type=F32)
            p = jnp.exp2(s)
            l_ref[...] += jnp.sum(p.reshape(tk // 8, 8, 2 * tq), axis=0)
            acc_ref[...] += jnp.dot(vb, p.astype(BF16), preferred_element_type=F32)
            return carry

        lax.fori_loop(0, n_kb, body, 0)

    @pl.when(bounded_ref[0] != 1)
    def _():
        m_ref[...] = jnp.full_like(m_ref, NEG_BIG)

        def body(j, carry):
            kb, vb = blocks(j)
            s = jnp.dot(kb, q2, preferred_element_type=F32)
            m_prev = m_ref[...]
            m_new = jnp.maximum(m_prev, jnp.max(s, axis=0, keepdims=True))
            alpha = jnp.exp2(m_prev - m_new)
            p = jnp.exp2(s - m_new)
            l_ref[...] = alpha * l_ref[...] + jnp.sum(p.reshape(tk // 8, 8, 2 * tq), axis=0)
            acc_ref[...] = acc_ref[...] * alpha + jnp.dot(vb, p.astype(BF16),
                                                          preferred_element_type=F32)
            m_ref[...] = m_new
            return carry

        lax.fori_loop(0, n_kb, body, 0)

    lam = (jnp.exp(jnp.sum(lq1_ref[...] * lk1_ref[...], axis=-1, keepdims=True))
           - jnp.exp(jnp.sum(lq2_ref[...] * lk2_ref[...], axis=-1, keepdims=True)) + lam_init)
    l = jnp.sum(l_ref[...], axis=0, keepdims=True)
    o = acc_ref[:, :tq] * (1.0 / l[:, :tq]) - acc_ref[:, tq:] * (lam / l[:, tq:])
    scale = lax.rsqrt(jnp.mean(o * o, axis=0, keepdims=True) + EPS)
    y = o * scale * (sub_ref[...] * (1.0 - lam_init))
    o_ref[0] = y.T.astype(o_ref.dtype)


def _attention(bounded, qT, k, vT, lq1, lk1, lq2, lk2, sub, *, lam_init, n_heads, tq, tk):
    batch, d_qk, seq = qT.shape
    d_v = vT.shape[1]
    hq, hv = d_qk // n_heads, d_v // n_heads
    vec = lambda b, h, i, flag: (0, 0)
    return pl.pallas_call(
        functools.partial(_attn_kernel, tk=tk, lam_init=lam_init),
        grid_spec=pltpu.PrefetchScalarGridSpec(
            num_scalar_prefetch=1,
            grid=(batch, n_heads, seq // tq),
            in_specs=[
                pl.BlockSpec((1, hq, tq), lambda b, h, i, flag: (b, h, i)),
                pl.BlockSpec((1, seq, hq), lambda b, h, i, flag: (b, 0, h)),
                pl.BlockSpec((1, hv, seq), lambda b, h, i, flag: (b, h, 0)),
                pl.BlockSpec(lq1.shape, vec),
                pl.BlockSpec(lk1.shape, vec),
                pl.BlockSpec(lq2.shape, vec),
                pl.BlockSpec(lk2.shape, vec),
                pl.BlockSpec(sub.shape, vec),
            ],
            out_specs=pl.BlockSpec((1, tq, hv), lambda b, h, i, flag: (b, i, h)),
            scratch_shapes=[
                pltpu.VMEM((1, 2 * tq), F32),
                pltpu.VMEM((8, 2 * tq), F32),
                pltpu.VMEM((hv, 2 * tq), F32),
            ],
        ),
        out_shape=jax.ShapeDtypeStruct((batch, seq, d_v), BF16),
        compiler_params=_params("parallel", "parallel", "arbitrary"),
        name="attn",
    )(bounded, qT, k, vT, lq1, lk1, lq2, lk2, sub)


def _merge_kernel(x_ref, f_ref, o_ref, sg_ref, pf_ref, pa_ref, wo_ref, out_ref):
    d = x_ref.shape[1]
    bf = jnp.dot(f_ref[...], pf_ref[...], preferred_element_type=F32)
    ba = jnp.dot(o_ref[...], pa_ref[...], preferred_element_type=F32)
    mix = sg_ref[:, :d].astype(F32) * bf + sg_ref[:, d:].astype(F32) * ba
    out_ref[...] = x_ref[...] + jnp.dot(mix.astype(BF16), wo_ref[...], preferred_element_type=F32)


def _merge(x, f, o, sg, p_f, p_a, w_o, *, layer, tm):
    m, d = x.shape
    row = lambda i: (i, 0)
    stacked = lambda w: pl.BlockSpec((None,) + w.shape[1:], lambda i: (layer, 0, 0),
                                     pipeline_mode=pl.Buffered(1))
    return pl.pallas_call(
        _merge_kernel,
        grid=(m // tm,),
        in_specs=[
            pl.BlockSpec((tm, d), row),
            pl.BlockSpec((tm, f.shape[1]), row),
            pl.BlockSpec((tm, o.shape[1]), row),
            pl.BlockSpec((tm, sg.shape[1]), row),
            stacked(p_f),
            stacked(p_a),
            stacked(w_o),
        ],
        out_specs=pl.BlockSpec((tm, d), row),
        out_shape=jax.ShapeDtypeStruct((m, d), F32),
        compiler_params=_params("parallel"),
        name="merge",
    )(x, f, o, sg, p_f, p_a, w_o)


def _dft_tables(seq, group_dim):
    s = jnp.arange(seq // 2, dtype=jnp.int32)[None, :]
    hi = jnp.arange(seq // V7X_LANES, dtype=jnp.int32)[:, None]
    lo = jnp.arange(V7X_LANES, dtype=jnp.int32)[:, None]
    ang_hi = ((hi * V7X_LANES * s) % seq).astype(F32) * (2.0 * math.pi / seq)
    ang_lo = ((lo * s) % seq).astype(F32) * (2.0 * math.pi / seq)
    ca, sa = jnp.cos(ang_hi)[:, None, :], jnp.sin(ang_hi)[:, None, :]
    cb, sb = jnp.cos(ang_lo)[None, :, :], jnp.sin(ang_lo)[None, :, :]
    k_par = (1 - 2 * (jnp.arange(seq, dtype=jnp.int32) % 2)).astype(F32)[:, None]
    col0 = s == 0
    cmat = jnp.where(col0, 0.5, (ca * cb - sa * sb).reshape(seq, seq // 2)).astype(BF16)
    smat = jnp.where(col0, k_par, (-(sa * cb + ca * sb)).reshape(seq, seq // 2)).astype(BF16)
    c = jnp.arange(group_dim, dtype=jnp.int32)
    angc = ((c[:, None] * c[None, :]) % group_dim).astype(F32) * (2.0 * math.pi / group_dim)
    ortho = 1.0 / math.sqrt(seq * group_dim)
    wc = (jnp.concatenate([jnp.cos(angc), jnp.sin(angc)], axis=1) * ortho).astype(BF16)
    return cmat, smat, wc


def _rope_lane_tables(seq, head_dim):
    rope_dim = head_dim // 4
    half = rope_dim // 2
    pos = jnp.arange(seq, dtype=F32)
    inv_freq = ROPE_THETA ** (-jnp.arange(0, rope_dim, 2, dtype=F32) / rope_dim)
    ang = pos[:, None] * inv_freq[None, :]
    cos, sin = jnp.cos(ang), jnp.sin(ang)
    ones = jnp.ones((seq, head_dim - rope_dim), F32)
    zeros_h = jnp.zeros((seq, half), F32)
    zeros_r = jnp.zeros((seq, head_dim - rope_dim), F32)
    rc = jnp.concatenate([cos, cos, ones], axis=1)
    rs1 = jnp.concatenate([-sin, zeros_h, zeros_r], axis=1)
    rs2 = jnp.concatenate([zeros_h, sin, zeros_r], axis=1)
    reps = V7X_LANES // head_dim
    return tuple(jnp.tile(t, (1, reps)) for t in (rc, rs1, rs2))


def kernel(x, norm_ffa, ffa_gate, ffa_up, ffa_down, norm_mix, w_in, q_norm, k_norm, lambda_q1,
           lambda_k1, lambda_q2, lambda_k2, subln, p_f, p_a, w_o, norm_ffb, ffb_gate, ffb_up,
           ffb_down, norm_out):
    batch, seq, d = x.shape
    depth = w_in.shape[0]
    head_dim = q_norm.shape[1]
    v_dim = subln.shape[1]
    d_f = p_f.shape[1]
    d_v = p_a.shape[1]
    n_heads = d_v // v_dim
    group_dim = d_f // N_FOURIER_GROUPS
    tn = d_f
    assert w_in.shape[2] == 4 * tn + 2 * d and d_v == tn and n_heads * 2 * head_dim == tn
    assert 2 * head_dim == V7X_LANES and v_dim == V7X_LANES

    m = batch * seq
    tm_ffn_a = min(1024, seq)
    tm_ffn_b = min(1024, seq)
    tm_in = min(1024, seq)
    tm_merge = min(512, seq)
    t_dft = min(1024, seq)
    t_fold = min(512, seq // 2)
    tq = min(2048, seq)
    tk = min(2048, seq)
    tf = 512

    cmat, smat, wc = _dft_tables(seq, group_dim)
    rc, rs1, rs2 = _rope_lane_tables(seq, head_dim)
    blk = jnp.arange(V7X_MXU_DIM, dtype=jnp.int32) // head_dim
    e = (blk[:, None] == blk[None, :]).astype(BF16)
    q_scale = head_dim ** -0.5 * LOG2E

    ffa_gate, ffa_up, ffa_down, w_in, p_f, p_a, w_o, ffb_gate, ffb_up, ffb_down = (
        w.astype(BF16) for w in (ffa_gate, ffa_up, ffa_down, w_in, p_f, p_a, w_o, ffb_gate, ffb_up,
                                 ffb_down))
    xs = x.reshape(m, d)
    for i in range(depth):
        lam_init = 0.8 - 0.6 * math.exp(-0.3 * i)
        xs, h = _ffn(xs, norm_ffa[i][None], ffa_gate, ffa_up, ffa_down, norm_mix[i][None], layer=i,
                     tail="norm_copy", tm=tm_ffn_a, tf=tf)
        qg = jnp.tile(q_norm[i], tn // head_dim)[None] * q_scale
        kg = jnp.tile(k_norm[i], tn // head_dim)[None]
        project = functools.partial(_project, h, w_in, layer=i, batch=batch, seq=seq, tm=tm_in,
                                    tn=tn, group_dim=group_dim, head_dim=head_dim)
        ab = project((wc,), kind="fourier", first_block=0, n_col_blocks=1)
        qT = project((e, qg, rc, rs1, rs2), kind="q", first_block=1, n_col_blocks=1)
        k = project((e, kg, rc, rs1, rs2), kind="k", first_block=2, n_col_blocks=1)
        vT = project((), kind="v", first_block=3, n_col_blocks=1)
        sg = project((), kind="gate", first_block=4, n_col_blocks=2 * d // tn)
        folded = _fold(ab.reshape(batch, seq, 2 * tn), t=t_fold)
        f = _fourier(cmat, smat, folded, tm=t_dft, tk=min(t_dft, seq // 2))
        score_bound = head_dim * q_scale * jnp.max(jnp.abs(q_norm[i])) * jnp.max(jnp.abs(k_norm[i]))
        bounded = (score_bound <= SCORE_BOUND_LIMIT).astype(jnp.int32).reshape(1)
        o = _attention(bounded, qT, k.reshape(batch, seq, tn), vT, lambda_q1[i][None],
                       lambda_k1[i][None], lambda_q2[i][None], lambda_k2[i][None],
                       subln[i][:, None], lam_init=lam_init, n_heads=n_heads, tq=tq, tk=tk)
        xs = _merge(xs, f.reshape(m, d_f), o.reshape(m, d_v), sg, p_f, p_a, w_o, layer=i,
                    tm=tm_merge)
        xs = _ffn(xs, norm_ffb[i][None], ffb_gate, ffb_up, ffb_down, norm_out[i][None], layer=i,
                  tail="norm_inplace", tm=tm_ffn_b, tf=tf)
    return xs.reshape(batch, seq, d)
```

```python
import functools
import math

import jax
import jax.numpy as jnp
from jax import lax
from jax.experimental import pallas as pl
from jax.experimental.pallas import tpu as pltpu

F32 = jnp.float32
BF16 = jnp.bfloat16

N_FOURIER_GROUPS = 4
ROPE_THETA = 500000.0
EPS = 1e-6
LOG2E = 1.4426950408889634
NEG_BIG = -1e30
SCORE_BOUND_LIMIT = 64.0

V7X_LANES = 128
V7X_MXU_DIM = 256
V7X_VMEM_LIMIT_BYTES = 60 * 1024 * 1024


def _params(*semantics):
    return pltpu.CompilerParams(dimension_semantics=semantics,
                                vmem_limit_bytes=V7X_VMEM_LIMIT_BYTES)


def _rms_scale(x):
    return lax.rsqrt(jnp.mean(x * x, axis=-1, keepdims=True) + EPS)


def _ffn_kernel(x_ref, gin_ref, wg_ref, wu_ref, wd_ref, gtail_ref, o_ref, h_ref, *, tail):
    j = pl.program_id(1)

    def chunk(h):
        g = jnp.dot(h, wg_ref[...], preferred_element_type=F32)
        u = jnp.dot(h, wu_ref[...], preferred_element_type=F32)
        a = (0.5 * g) * jax.nn.sigmoid(g) * u
        return jnp.dot(a.astype(BF16), wd_ref[...], preferred_element_type=F32)

    @pl.when(j == 0)
    def _():
        x = x_ref[...]
        h = (x * _rms_scale(x) * gin_ref[...]).astype(BF16)
        h_ref[...] = h
        o_ref[...] = x + chunk(h)

    @pl.when(j > 0)
    def _():
        o_ref[...] += chunk(h_ref[...])

    @pl.when(j == pl.num_programs(1) - 1)
    def _():
        y = o_ref[...]
        yn = y * _rms_scale(y) * gtail_ref[...]
        if tail == "norm_inplace":
            o_ref[...] = yn
        else:
            h_ref[...] = yn.astype(BF16)


def _ffn(x, g_in, wg, wu, wd, g_tail, *, layer, tail, tm, tf):
    m, d = x.shape
    n_chunks = wg.shape[2] // tf
    tile = pl.BlockSpec((tm, d), lambda i, j: (i, 0))
    copy = tail == "norm_copy"
    return pl.pallas_call(
        functools.partial(_ffn_kernel, tail=tail),
        grid=(m // tm, n_chunks),
        in_specs=[
            tile,
            pl.BlockSpec((1, d), lambda i, j: (0, 0)),
            pl.BlockSpec((None, d, tf), lambda i, j: (layer, 0, j)),
            pl.BlockSpec((None, d, tf), lambda i, j: (layer, 0, j)),
            pl.BlockSpec((None, tf, d), lambda i, j: (layer, j, 0)),
            pl.BlockSpec((1, d), lambda i, j: (0, 0)),
        ],
        out_specs=[tile, tile] if copy else tile,
        out_shape=([jax.ShapeDtypeStruct((m, d), F32), jax.ShapeDtypeStruct((m, d), BF16)] if copy
                   else jax.ShapeDtypeStruct((m, d), F32)),
        scratch_shapes=[] if copy else [pltpu.VMEM((tm, d), BF16)],
        compiler_params=_params("parallel", "arbitrary"),
        name="ffn",
    )(x, g_in, wg, wu, wd, g_tail)


def _norm_rope(z, e_ref, gain, rc, rs1, rs2, head_dim):
    tn = z.shape[1]
    zz = (z * z).astype(BF16)
    half = head_dim // 8
    outs = []
    for c in range(tn // V7X_MXU_DIM):
        cols = slice(c * V7X_MXU_DIM, (c + 1) * V7X_MXU_DIM)
        ss = jnp.dot(zz[:, cols], e_ref[...], preferred_element_type=F32)
        y = z[:, cols] * lax.rsqrt(ss * (1.0 / head_dim) + EPS) * gain[:, cols]
        for hh in range(V7X_MXU_DIM // V7X_LANES):
            yc = y[:, hh * V7X_LANES:(hh + 1) * V7X_LANES]
            outs.append(yc * rc + pltpu.roll(yc, V7X_LANES - half, 1) * rs1
                        + pltpu.roll(yc, half, 1) * rs2)
    return jnp.concatenate(outs, axis=1)


def _proj_kernel(*refs, kind, n_tiles, group_dim, head_dim):
    h_ref, w_ref = refs[:2]
    aux, out_ref, z_ref = refs[2:-2], refs[-2], refs[-1]
    t = pl.program_id(1)

    def project(slot):
        z_ref[slot] = jnp.dot(h_ref[...], w_ref[...], preferred_element_type=F32)

    def epilogue(z):
        if kind == "fourier":
            wc_ref, = aux
            u = z.astype(BF16)
            n_groups = z.shape[1] // group_dim
            for g in range(n_groups):
                c = jnp.dot(u[:, g * group_dim:(g + 1) * group_dim], wc_ref[...],
                            preferred_element_type=F32)
                out_ref[:, g * group_dim:(g + 1) * group_dim] = c[:, :group_dim].astype(BF16)
                out_ref[:, (n_groups + g) * group_dim:(n_groups + g + 1) * group_dim] = (
                    c[:, group_dim:].astype(BF16))
        elif kind in ("q", "k"):
            e_ref, gain_ref, rc_ref, rs1_ref, rs2_ref = aux
            y = _norm_rope(z, e_ref, gain_ref[...], rc_ref[...], rs1_ref[...], rs2_ref[...],
                           head_dim)
            if kind == "q":
                out_ref[0] = y.T.astype(BF16)
            else:
                out_ref[...] = y.astype(BF16)
        elif kind == "v":
            out_ref[0] = z.T.astype(BF16)
        else:
            out_ref[...] = z.astype(BF16)

    @pl.when(t == 0)
    def _():
        project(0)

    for parity in range(2):
        @pl.when(jnp.logical_and(jnp.logical_and(t > 0, t < n_tiles), t % 2 == parity))
        def _(parity=parity):
            epilogue(z_ref[1 - parity])
            project(parity)

    @pl.when(t == n_tiles)
    def _():
        epilogue(z_ref[(n_tiles - 1) % 2])


def _project(h, w_in, aux, *, kind, layer, first_block, n_col_blocks, batch, seq, tm, tn, group_dim,
             head_dim):
    m, d = h.shape
    n_tiles, n_s = m // tm, seq // tm
    done = lambda t: jnp.maximum(t - 1, 0)
    const = lambda c, t: (0, 0)
    pos = lambda c, t: (done(t) % n_s, 0)
    aux_specs = {
        "fourier": [pl.BlockSpec(a.shape, const) for a in aux],
        "q": [pl.BlockSpec(a.shape, const) for a in aux[:2]]
             + [pl.BlockSpec((tm, V7X_LANES), pos) for _ in aux[2:]],
        "v": [],
        "gate": [],
    }
    aux_specs["k"] = aux_specs["q"]
    if kind == "fourier":
        out_shape, out_spec = (m, 2 * tn), pl.BlockSpec((tm, 2 * tn), lambda c, t: (done(t), 0))
    elif kind in ("q", "v"):
        out_shape = (batch, tn, seq)
        out_spec = pl.BlockSpec((1, tn, tm), lambda c, t: (done(t) // n_s, 0, done(t) % n_s))
    else:
        out_shape = (m, n_col_blocks * tn)
        out_spec = pl.BlockSpec((tm, tn), lambda c, t: (done(t), c))
    return pl.pallas_call(
        functools.partial(_proj_kernel, kind=kind, n_tiles=n_tiles, group_dim=group_dim,
                          head_dim=head_dim),
        grid=(n_col_blocks, n_tiles + 1),
        in_specs=[
            pl.BlockSpec((tm, d), lambda c, t: (jnp.minimum(t, n_tiles - 1), 0)),
            pl.BlockSpec((None, d, tn), lambda c, t: (layer, 0, first_block + c)),
        ] + aux_specs[kind],
        out_specs=out_spec,
        out_shape=jax.ShapeDtypeStruct(out_shape, BF16),
        scratch_shapes=[pltpu.VMEM((2, tm, tn), F32)],
        compiler_params=_params("arbitrary", "arbitrary"),
        name="proj_" + kind,
    )(h, w_in, *aux)


def _fold_kernel(anti_ref, pick_ref, lo_ref, hi_ref, above_ref, mid_ref, o_ref):
    w = lo_ref.shape[2] // 2
    first = jnp.where(pl.program_id(0) == 0, 1.0, 0.0)
    for b in range(lo_ref.shape[0]):
        rev = (jnp.dot(anti_ref[...], hi_ref[b], preferred_element_type=F32)
               + jnp.dot(pick_ref[...], above_ref[b], preferred_element_type=F32))
        mid = jnp.dot(pick_ref[...], mid_ref[b], preferred_element_type=F32)
        lo = lo_ref[b].astype(F32)
        o_ref[b, :, :w] = (lo[:, :w] + rev[:, :w]).astype(BF16)
        o_ref[b, :, w:] = (lo[:, w:] - rev[:, w:] + first * mid[:, :w]).astype(BF16)


def _fold(ab, *, t):
    batch, seq, two_w = ab.shape
    nb = seq // t
    sub = 16
    r = jnp.arange(t, dtype=jnp.int32)
    anti = (r[:, None] + r[None, :] == t).astype(BF16)
    pick = (r[:, None] + jnp.arange(sub, dtype=jnp.int32)[None, :] == 0).astype(BF16)
    return pl.pallas_call(
        _fold_kernel,
        grid=(nb // 2,),
        in_specs=[
            pl.BlockSpec((t, t), lambda i: (0, 0)),
            pl.BlockSpec((t, sub), lambda i: (0, 0)),
            pl.BlockSpec((batch, t, two_w), lambda i: (0, i, 0)),
            pl.BlockSpec((batch, t, two_w), lambda i: (0, nb - 1 - i, 0)),
            pl.BlockSpec((batch, sub, two_w), lambda i: (0, ((nb - i) * (t // sub)) % (seq // sub), 0)),
            pl.BlockSpec((batch, sub, two_w), lambda i: (0, seq // 2 // sub, 0)),
        ],
        out_specs=pl.BlockSpec((batch, t, two_w), lambda i: (0, i, 0)),
        out_shape=jax.ShapeDtypeStruct((batch, seq // 2, two_w), BF16),
        compiler_params=_params("parallel"),
        name="fold",
    )(anti, pick, ab, ab, ab, ab)


def _fourier_kernel(c_ref, s_ref, a_ref, b_ref, o_ref, acc_ref):
    kk = pl.program_id(1)

    @pl.when(kk == 0)
    def _():
        acc_ref[...] = jnp.zeros_like(acc_ref)

    for b in range(a_ref.shape[0]):
        acc_ref[b] += (jnp.dot(c_ref[...], a_ref[b], preferred_element_type=F32)
                       + jnp.dot(s_ref[...], b_ref[b], preferred_element_type=F32))

    @pl.when(kk == pl.num_programs(1) - 1)
    def _():
        o_ref[...] = acc_ref[...].astype(o_ref.dtype)


def _fourier(cmat, smat, folded, *, tm, tk):
    batch, half_seq, two_w = folded.shape
    seq = cmat.shape[0]
    w = two_w // 2
    return pl.pallas_call(
        _fourier_kernel,
        grid=(seq // tm, half_seq // tk),
        in_specs=[
            pl.BlockSpec((tm, tk), lambda i, kk: (i, kk)),
            pl.BlockSpec((tm, tk), lambda i, kk: (i, kk)),
            pl.BlockSpec((batch, tk, w), lambda i, kk: (0, kk, 0)),
            pl.BlockSpec((batch, tk, w), lambda i, kk: (0, kk, 1)),
        ],
        out_specs=pl.BlockSpec((batch, tm, w), lambda i, kk: (0, i, 0)),
        out_shape=jax.ShapeDtypeStruct((batch, seq, w), BF16),
        scratch_shapes=[pltpu.VMEM((batch, tm, w), F32)],
        compiler_params=_params("parallel", "arbitrary"),
        name="fourier",
    )(cmat, smat, folded, folded)


def _attn_kernel(bounded_ref, qT_ref, k_ref, vT_ref, lq1_ref, lk1_ref, lq2_ref, lk2_ref, sub_ref,
                 o_ref, m_ref, l_ref, acc_ref, *, tk, lam_init):
    qT = qT_ref[0]
    qk_dim = qT.shape[0] // 2
    tq = qT.shape[1]
    row = lax.broadcasted_iota(jnp.int32, qT.shape, 0)
    zero = jnp.zeros_like(qT)
    q2 = jnp.concatenate([jnp.where(row < qk_dim, qT, zero), jnp.where(row >= qk_dim, qT, zero)],
                         axis=1)
    n_kb = k_ref.shape[1] // tk

    l_ref[...] = jnp.zeros_like(l_ref)
    acc_ref[...] = jnp.zeros_like(acc_ref)

    def blocks(j):
        start = pl.multiple_of(j * tk, tk)
        return k_ref[0, pl.ds(start, tk), :], vT_ref[0, :, pl.ds(start, tk)]

    @pl.when(bounded_ref[0] == 1)
    def _():
        def body(j, carry):
            kb, vb = blocks(j)
            s = jnp.dot(kb, q2, preferred_element_type=F32)
            p = jnp.exp2(s)
            l_ref[...] += jnp.sum(p.reshape(tk // 8, 8, 2 * tq), axis=0)
            acc_ref[...] += jnp.dot(vb, p.astype(BF16), preferred_element_type=F32)
            return carry

        lax.fori_loop(0, n_kb, body, 0)

    @pl.when(bounded_ref[0] != 1)
    def _():
        m_ref[...] = jnp.full_like(m_ref, NEG_BIG)

        def body(j, carry):
            kb, vb = blocks(j)
            s = jnp.dot(kb, q2, preferred_element_type=F32)
            m_prev = m_ref[...]
            m_new = jnp.maximum(m_prev, jnp.max(s, axis=0, keepdims=True))
            alpha = jnp.exp2(m_prev - m_new)
            p = jnp.exp2(s - m_new)
            l_ref[...] = alpha * l_ref[...] + jnp.sum(p.reshape(tk // 8, 8, 2 * tq), axis=0)
            acc_ref[...] = acc_ref[...] * alpha + jnp.dot(vb, p.astype(BF16),
                                                          preferred_element_type=F32)
            m_ref[...] = m_new
            return carry

        lax.fori_loop(0, n_kb, body, 0)

    lam = (jnp.exp(jnp.sum(lq1_ref[...] * lk1_ref[...], axis=-1, keepdims=True))
           - jnp.exp(jnp.sum(lq2_ref[...] * lk2_ref[...], axis=-1, keepdims=True)) + lam_init)
    l = jnp.sum(l_ref[...], axis=0, keepdims=True)
    o = acc_ref[:, :tq] * (1.0 / l[:, :tq]) - acc_ref[:, tq:] * (lam / l[:, tq:])
    scale = lax.rsqrt(jnp.mean(o * o, axis=0, keepdims=True) + EPS)
    y = o * scale * (sub_ref[...] * (1.0 - lam_init))
    o_ref[0] = y.T.astype(o_ref.dtype)


def _attention(bounded, qT, k, vT, lq1, lk1, lq2, lk2, sub, *, lam_init, n_heads, tq, tk):
    batch, d_qk, seq = qT.shape
    d_v = vT.shape[1]
    hq, hv = d_qk // n_heads, d_v // n_heads
    vec = lambda b, h, i, flag: (0, 0)
    return pl.pallas_call(
        functools.partial(_attn_kernel, tk=tk, lam_init=lam_init),
        grid_spec=pltpu.PrefetchScalarGridSpec(
            num_scalar_prefetch=1,
            grid=(batch, n_heads, seq // tq),
            in_specs=[
                pl.BlockSpec((1, hq, tq), lambda b, h, i, flag: (b, h, i)),
                pl.BlockSpec((1, seq, hq), lambda b, h, i, flag: (b, 0, h)),
                pl.BlockSpec((1, hv, seq), lambda b, h, i, flag: (b, h, 0)),
                pl.BlockSpec(lq1.shape, vec),
                pl.BlockSpec(lk1.shape, vec),
                pl.BlockSpec(lq2.shape, vec),
                pl.BlockSpec(lk2.shape, vec),
                pl.BlockSpec(sub.shape, vec),
            ],
            out_specs=pl.BlockSpec((1, tq, hv), lambda b, h, i, flag: (b, i, h)),
            scratch_shapes=[
                pltpu.VMEM((1, 2 * tq), F32),
                pltpu.VMEM((8, 2 * tq), F32),
                pltpu.VMEM((hv, 2 * tq), F32),
            ],
        ),
        out_shape=jax.ShapeDtypeStruct((batch, seq, d_v), BF16),
        compiler_params=_params("parallel", "parallel", "arbitrary"),
        name="attn",
    )(bounded, qT, k, vT, lq1, lk1, lq2, lk2, sub)


def _merge_kernel(x_ref, f_ref, o_ref, sg_ref, pf_ref, pa_ref, wo_ref, out_ref):
    d = x_ref.shape[1]
    bf = jnp.dot(f_ref[...], pf_ref[...], preferred_element_type=F32)
    ba = jnp.dot(o_ref[...], pa_ref[...], preferred_element_type=F32)
    mix = (jax.nn.sigmoid(sg_ref[:, :d].astype(F32)) * bf
           + jax.nn.sigmoid(sg_ref[:, d:].astype(F32)) * ba)
    out_ref[...] = x_ref[...] + jnp.dot(mix.astype(BF16), wo_ref[...], preferred_element_type=F32)


def _merge(x, f, o, sg, p_f, p_a, w_o, *, layer, tm):
    m, d = x.shape
    row = lambda i: (i, 0)
    stacked = lambda w: pl.BlockSpec((None,) + w.shape[1:], lambda i: (layer, 0, 0),
                                     pipeline_mode=pl.Buffered(1))
    return pl.pallas_call(
        _merge_kernel,
        grid=(m // tm,),
        in_specs=[
            pl.BlockSpec((tm, d), row),
            pl.BlockSpec((tm, f.shape[1]), row),
            pl.BlockSpec((tm, o.shape[1]), row),
            pl.BlockSpec((tm, sg.shape[1]), row),
            stacked(p_f),
            stacked(p_a),
            stacked(w_o),
        ],
        out_specs=pl.BlockSpec((tm, d), row),
        out_shape=jax.ShapeDtypeStruct((m, d), F32),
        compiler_params=_params("parallel"),
        name="merge",
    )(x, f, o, sg, p_f, p_a, w_o)


def _dft_tables(seq, group_dim):
    s = jnp.arange(seq // 2, dtype=jnp.int32)[None, :]
    hi = jnp.arange(seq // V7X_LANES, dtype=jnp.int32)[:, None]
    lo = jnp.arange(V7X_LANES, dtype=jnp.int32)[:, None]
    ang_hi = ((hi * V7X_LANES * s) % seq).astype(F32) * (2.0 * math.pi / seq)
    ang_lo = ((lo * s) % seq).astype(F32) * (2.0 * math.pi / seq)
    ca, sa = jnp.cos(ang_hi)[:, None, :], jnp.sin(ang_hi)[:, None, :]
    cb, sb = jnp.cos(ang_lo)[None, :, :], jnp.sin(ang_lo)[None, :, :]
    k_par = (1 - 2 * (jnp.arange(seq, dtype=jnp.int32) % 2)).astype(F32)[:, None]
    col0 = s == 0
    cmat = jnp.where(col0, 0.5, (ca * cb - sa * sb).reshape(seq, seq // 2)).astype(BF16)
    smat = jnp.where(col0, k_par, (-(sa * cb + ca * sb)).reshape(seq, seq // 2)).astype(BF16)
    c = jnp.arange(group_dim, dtype=jnp.int32)
    angc = ((c[:, None] * c[None, :]) % group_dim).astype(F32) * (2.0 * math.pi / group_dim)
    ortho = 1.0 / math.sqrt(seq * group_dim)
    wc = (jnp.concatenate([jnp.cos(angc), jnp.sin(angc)], axis=1) * ortho).astype(BF16)
    return cmat, smat, wc


def _rope_lane_tables(seq, head_dim):
    rope_dim = head_dim // 4
    half = rope_dim // 2
    pos = jnp.arange(seq, dtype=F32)
    inv_freq = ROPE_THETA ** (-jnp.arange(0, rope_dim, 2, dtype=F32) / rope_dim)
    ang = pos[:, None] * inv_freq[None, :]
    cos, sin = jnp.cos(ang), jnp.sin(ang)
    ones = jnp.ones((seq, head_dim - rope_dim), F32)
    zeros_h = jnp.zeros((seq, half), F32)
    zeros_r = jnp.zeros((seq, head_dim - rope_dim), F32)
    rc = jnp.concatenate([cos, cos, ones], axis=1)
    rs1 = jnp.concatenate([-sin, zeros_h, zeros_r], axis=1)
    rs2 = jnp.concatenate([zeros_h, sin, zeros_r], axis=1)
    reps = V7X_LANES // head_dim
    return tuple(jnp.tile(t, (1, reps)) for t in (rc, rs1, rs2))


def kernel(x, norm_ffa, ffa_gate, ffa_up, ffa_down, norm_mix, w_in, q_norm, k_norm, lambda_q1,
           lambda_k1, lambda_q2, lambda_k2, subln, p_f, p_a, w_o, norm_ffb, ffb_gate, ffb_up,
           ffb_down, norm_out):
    batch, seq, d = x.shape
    depth = w_in.shape[0]
    head_dim = q_norm.shape[1]
    v_dim = subln.shape[1]
    d_f = p_f.shape[1]
    d_v = p_a.shape[1]
    n_heads = d_v // v_dim
    group_dim = d_f // N_FOURIER_GROUPS
    tn = d_f
    assert w_in.shape[2] == 4 * tn + 2 * d and d_v == tn and n_heads * 2 * head_dim == tn
    assert 2 * head_dim == V7X_LANES and v_dim == V7X_LANES

    m = batch * seq
    tm_ffn_a = min(1024, seq)
    tm_ffn_b = min(1024, seq)
    tm_in = min(1024, seq)
    tm_merge = min(512, seq)
    t_dft = min(1024, seq)
    t_fold = min(512, seq // 2)
    tq = min(2048, seq)
    tk = min(2048, seq)
    tf = 512

    cmat, smat, wc = _dft_tables(seq, group_dim)
    rc, rs1, rs2 = _rope_lane_tables(seq, head_dim)
    blk = jnp.arange(V7X_MXU_DIM, dtype=jnp.int32) // head_dim
    e = (blk[:, None] == blk[None, :]).astype(BF16)
    q_scale = head_dim ** -0.5 * LOG2E

    ffa_gate, ffa_up, ffa_down, w_in, p_f, p_a, w_o, ffb_gate, ffb_up, ffb_down = (
        w.astype(BF16) for w in (ffa_gate, ffa_up, ffa_down, w_in, p_f, p_a, w_o, ffb_gate, ffb_up,
                                 ffb_down))
    xs = x.reshape(m, d)
    for i in range(depth):
        lam_init = 0.8 - 0.6 * math.exp(-0.3 * i)
        xs, h = _ffn(xs, norm_ffa[i][None], ffa_gate, ffa_up, ffa_down, norm_mix[i][None], layer=i,
                     tail="norm_copy", tm=tm_ffn_a, tf=tf)
        qg = jnp.tile(q_norm[i], tn // head_dim)[None] * q_scale
        kg = jnp.tile(k_norm[i], tn // head_dim)[None]
        project = functools.partial(_project, h, w_in, layer=i, batch=batch, seq=seq, tm=tm_in,
                                    tn=tn, group_dim=group_dim, head_dim=head_dim)
        ab = project((wc,), kind="fourier", first_block=0, n_col_blocks=1)
        qT = project((e, qg, rc, rs1, rs2), kind="q", first_block=1, n_col_blocks=1)
        k = project((e, kg, rc, rs1, rs2), kind="k", first_block=2, n_col_blocks=1)
        vT = project((), kind="v", first_block=3, n_col_blocks=1)
        sg = project((), kind="gate", first_block=4, n_col_blocks=2 * d // tn)
        folded = _fold(ab.reshape(batch, seq, 2 * tn), t=t_fold)
        f = _fourier(cmat, smat, folded, tm=t_dft, tk=min(t_dft, seq // 2))
        score_bound = head_dim * q_scale * jnp.max(jnp.abs(q_norm[i])) * jnp.max(jnp.abs(k_norm[i]))
        bounded = (score_bound <= SCORE_BOUND_LIMIT).astype(jnp.int32).reshape(1)
        o = _attention(bounded, qT, k.reshape(batch, seq, tn), vT, lambda_q1[i][None],
                       lambda_k1[i][None], lambda_q2[i][None], lambda_k2[i][None],
                       subln[i][:, None], lam_init=lam_init, n_heads=n_heads, tq=tq, tk=tk)
        xs = _merge(xs, f.reshape(m, d_f), o.reshape(m, d_v), sg, p_f, p_a, w_o, layer=i,
                    tm=tm_merge)
        xs = _ffn(xs, norm_ffb[i][None], ffb_gate, ffb_up, ffb_down, norm_out[i][None], layer=i,
                  tail="norm_inplace", tm=tm_ffn_b, tf=tf)
    return xs.reshape(batch, seq, d)
```

```python
import functools
import math

import jax
import jax.numpy as jnp
from jax import lax
from jax.experimental import pallas as pl
from jax.experimental.pallas import tpu as pltpu

F32 = jnp.float32
BF16 = jnp.bfloat16

N_FOURIER_GROUPS = 4
ROPE_THETA = 500000.0
EPS = 1e-6
LOG2E = 1.4426950408889634
NEG_BIG = -1e30
SCORE_BOUND_LIMIT = 64.0

NYQUIST_ROWS = 16

V7X_LANES = 128
V7X_MXU_DIM = 256
V7X_VMEM_LIMIT_BYTES = 60 * 1024 * 1024


def _params(*semantics):
    return pltpu.CompilerParams(dimension_semantics=semantics,
                                vmem_limit_bytes=V7X_VMEM_LIMIT_BYTES)


def _rms_scale(x):
    return lax.rsqrt(jnp.mean(x * x, axis=-1, keepdims=True) + EPS)


def _ffn_kernel(x_ref, gin_ref, wg_ref, wu_ref, wd_ref, gtail_ref, o_ref, h_ref, *, tail):
    j = pl.program_id(1)

    def chunk(h):
        g = jnp.dot(h, wg_ref[...], preferred_element_type=F32)
        u = jnp.dot(h, wu_ref[...], preferred_element_type=F32)
        a = (0.5 * g) * jax.nn.sigmoid(g) * u
        return jnp.dot(a.astype(BF16), wd_ref[...], preferred_element_type=F32)

    @pl.when(j == 0)
    def _():
        x = x_ref[...]
        h = (x * _rms_scale(x) * gin_ref[...]).astype(BF16)
        h_ref[...] = h
        o_ref[...] = x + chunk(h)

    @pl.when(j > 0)
    def _():
        o_ref[...] += chunk(h_ref[...])

    @pl.when(j == pl.num_programs(1) - 1)
    def _():
        y = o_ref[...]
        yn = y * _rms_scale(y) * gtail_ref[...]
        if tail == "norm_inplace":
            o_ref[...] = yn
        else:
            h_ref[...] = yn.astype(BF16)


def _ffn(x, g_in, wg, wu, wd, g_tail, *, layer, tail, tm, tf):
    m, d = x.shape
    n_chunks = wg.shape[2] // tf
    tile = pl.BlockSpec((tm, d), lambda i, j: (i, 0))
    copy = tail == "norm_copy"
    return pl.pallas_call(
        functools.partial(_ffn_kernel, tail=tail),
        grid=(m // tm, n_chunks),
        in_specs=[
            tile,
            pl.BlockSpec((1, d), lambda i, j: (0, 0)),
            pl.BlockSpec((None, d, tf), lambda i, j: (layer, 0, j)),
            pl.BlockSpec((None, d, tf), lambda i, j: (layer, 0, j)),
            pl.BlockSpec((None, tf, d), lambda i, j: (layer, j, 0)),
            pl.BlockSpec((1, d), lambda i, j: (0, 0)),
        ],
        out_specs=[tile, tile] if copy else tile,
        out_shape=([jax.ShapeDtypeStruct((m, d), F32), jax.ShapeDtypeStruct((m, d), BF16)] if copy
                   else jax.ShapeDtypeStruct((m, d), F32)),
        scratch_shapes=[] if copy else [pltpu.VMEM((tm, d), BF16)],
        compiler_params=_params("parallel", "arbitrary"),
        name="ffn",
    )(x, g_in, wg, wu, wd, g_tail)


def _norm_rope(z, e_ref, gain, rc, rs1, rs2, head_dim):
    tn = z.shape[1]
    zz = (z * z).astype(BF16)
    half = head_dim // 8
    outs = []
    for c in range(tn // V7X_MXU_DIM):
        cols = slice(c * V7X_MXU_DIM, (c + 1) * V7X_MXU_DIM)
        ss = jnp.dot(zz[:, cols], e_ref[...], preferred_element_type=F32)
        y = z[:, cols] * lax.rsqrt(ss * (1.0 / head_dim) + EPS) * gain[:, cols]
        for hh in range(V7X_MXU_DIM // V7X_LANES):
            yc = y[:, hh * V7X_LANES:(hh + 1) * V7X_LANES]
            outs.append(yc * rc + pltpu.roll(yc, V7X_LANES - half, 1) * rs1
                        + pltpu.roll(yc, half, 1) * rs2)
    return jnp.concatenate(outs, axis=1)


def _proj_kernel(*refs, kind, n_tiles, group_dim, head_dim):
    h_ref, w_ref = refs[:2]
    aux, out_ref, z_ref = refs[2:-2], refs[-2], refs[-1]
    t = pl.program_id(1)

    def project(slot):
        z_ref[slot] = jnp.dot(h_ref[...], w_ref[...], preferred_element_type=F32)

    def epilogue(z):
        if kind == "fourier":
            wc_ref, = aux
            u = z.astype(BF16)
            n_groups = z.shape[1] // group_dim
            for g in range(n_groups):
                c = jnp.dot(u[:, g * group_dim:(g + 1) * group_dim], wc_ref[...],
                            preferred_element_type=F32)
                out_ref[:, g * group_dim:(g + 1) * group_dim] = c[:, :group_dim].astype(BF16)
                out_ref[:, (n_groups + g) * group_dim:(n_groups + g + 1) * group_dim] = (
                    c[:, group_dim:].astype(BF16))
        elif kind in ("q", "k"):
            e_ref, gain_ref, rc_ref, rs1_ref, rs2_ref = aux
            y = _norm_rope(z, e_ref, gain_ref[...], rc_ref[...], rs1_ref[...], rs2_ref[...],
                           head_dim)
            if kind == "q":
                out_ref[0] = y.T.astype(BF16)
            else:
                out_ref[...] = y.astype(BF16)
        elif kind == "v":
            out_ref[0] = z.T.astype(BF16)
        else:
            out_ref[...] = z.astype(BF16)

    @pl.when(t == 0)
    def _():
        project(0)

    for parity in range(2):
        @pl.when(jnp.logical_and(jnp.logical_and(t > 0, t < n_tiles), t % 2 == parity))
        def _(parity=parity):
            epilogue(z_ref[1 - parity])
            project(parity)

    @pl.when(t == n_tiles)
    def _():
        epilogue(z_ref[(n_tiles - 1) % 2])


def _project(h, w_in, aux, *, kind, layer, first_block, n_col_blocks, batch, seq, tm, tn, group_dim,
             head_dim):
    m, d = h.shape
    n_tiles, n_s = m // tm, seq // tm
    done = lambda t: jnp.maximum(t - 1, 0)
    const = lambda c, t: (0, 0)
    pos = lambda c, t: (done(t) % n_s, 0)
    aux_specs = {
        "fourier": [pl.BlockSpec(a.shape, const) for a in aux],
        "q": [pl.BlockSpec(a.shape, const) for a in aux[:2]]
             + [pl.BlockSpec((tm, V7X_LANES), pos) for _ in aux[2:]],
        "v": [],
        "gate": [],
    }
    aux_specs["k"] = aux_specs["q"]
    if kind == "fourier":
        out_shape, out_spec = (m, 2 * tn), pl.BlockSpec((tm, 2 * tn), lambda c, t: (done(t), 0))
    elif kind in ("q", "v"):
        out_shape = (batch, tn, seq)
        out_spec = pl.BlockSpec((1, tn, tm), lambda c, t: (done(t) // n_s, 0, done(t) % n_s))
    else:
        out_shape = (m, n_col_blocks * tn)
        out_spec = pl.BlockSpec((tm, tn), lambda c, t: (done(t), c))
    return pl.pallas_call(
        functools.partial(_proj_kernel, kind=kind, n_tiles=n_tiles, group_dim=group_dim,
                          head_dim=head_dim),
        grid=(n_col_blocks, n_tiles + 1),
        in_specs=[
            pl.BlockSpec((tm, d), lambda c, t: (jnp.minimum(t, n_tiles - 1), 0)),
            pl.BlockSpec((None, d, tn), lambda c, t: (layer, 0, first_block + c)),
        ] + aux_specs[kind],
        out_specs=out_spec,
        out_shape=jax.ShapeDtypeStruct(out_shape, BF16),
        scratch_shapes=[pltpu.VMEM((2, tm, tn), F32)],
        compiler_params=_params("arbitrary", "arbitrary"),
        name="proj_" + kind,
    )(h, w_in, *aux)


def _fold_kernel(anti_ref, pick_ref, lo_ref, hi_ref, above_ref, mid_ref, o_ref):
    w = lo_ref.shape[2] // 2
    first = jnp.where(pl.program_id(0) == 0, 1.0, 0.0)
    for b in range(lo_ref.shape[0]):
        rev = (jnp.dot(anti_ref[...], hi_ref[b], preferred_element_type=F32)
               + jnp.dot(pick_ref[...], above_ref[b], preferred_element_type=F32))
        mid = jnp.dot(pick_ref[...], mid_ref[b], preferred_element_type=F32)
        lo = lo_ref[b].astype(F32)
        o_ref[b, :, :w] = (lo[:, :w] + rev[:, :w]).astype(BF16)
        o_ref[b, :, w:] = (lo[:, w:] - rev[:, w:] + first * mid[:, :w]).astype(BF16)


def _fold(ab, *, t):
    batch, seq, two_w = ab.shape
    nb = seq // t
    sub = 16
    r = jnp.arange(t, dtype=jnp.int32)
    anti = (r[:, None] + r[None, :] == t).astype(BF16)
    pick = (r[:, None] + jnp.arange(sub, dtype=jnp.int32)[None, :] == 0).astype(BF16)
    return pl.pallas_call(
        _fold_kernel,
        grid=(nb // 2,),
        in_specs=[
            pl.BlockSpec((t, t), lambda i: (0, 0)),
            pl.BlockSpec((t, sub), lambda i: (0, 0)),
            pl.BlockSpec((batch, t, two_w), lambda i: (0, i, 0)),
            pl.BlockSpec((batch, t, two_w), lambda i: (0, nb - 1 - i, 0)),
            pl.BlockSpec((batch, sub, two_w), lambda i: (0, ((nb - i) * (t // sub)) % (seq // sub), 0)),
            pl.BlockSpec((batch, sub, two_w), lambda i: (0, seq // 2 // sub, 0)),
        ],
        out_specs=pl.BlockSpec((batch, t, two_w), lambda i: (0, i, 0)),
        out_shape=jax.ShapeDtypeStruct((batch, seq // 2, two_w), BF16),
        compiler_params=_params("parallel"),
        name="fold",
    )(anti, pick, ab, ab, ab, ab)


def _fourier_kernel(c_ref, s_ref, a_ref, b_ref, o_ref, acc_ref):
    kk = pl.program_id(1)

    @pl.when(kk == 0)
    def _():
        acc_ref[...] = jnp.zeros_like(acc_ref)

    for b in range(a_ref.shape[0]):
        acc_ref[b] += (jnp.dot(c_ref[...], a_ref[b], preferred_element_type=F32)
                       + jnp.dot(s_ref[...], b_ref[b], preferred_element_type=F32))

    @pl.when(kk == pl.num_programs(1) - 1)
    def _():
        o_ref[...] = acc_ref[...].astype(o_ref.dtype)


def _fourier(cmat, smat, folded, *, first_row, n_rows, out_rows, tm, tk):
    batch, half_seq, two_w = folded.shape
    w = two_w // 2
    row0 = first_row // tm
    return pl.pallas_call(
        _fourier_kernel,
        grid=(n_rows // tm, half_seq // tk),
        in_specs=[
            pl.BlockSpec((tm, tk), lambda i, kk: (row0 + i, kk)),
            pl.BlockSpec((tm, tk), lambda i, kk: (row0 + i, kk)),
            pl.BlockSpec((batch, tk, w), lambda i, kk: (0, kk, 0)),
            pl.BlockSpec((batch, tk, w), lambda i, kk: (0, kk, 1)),
        ],
        out_specs=pl.BlockSpec((batch, tm, w), lambda i, kk: (0, i, 0)),
        out_shape=jax.ShapeDtypeStruct((batch, out_rows, w), BF16),
        scratch_shapes=[pltpu.VMEM((batch, tm, w), F32)],
        compiler_params=_params("parallel", "arbitrary"),
        name="fourier",
    )(cmat, smat, folded, folded)


def _unfold_kernel(anti_ref, pick_ref, perm_ref, mirror_ref, above_ref, nyq_ref, o_ref):
    group = perm_ref.shape[0]
    first = jnp.where(pl.program_id(0) == 0, 1.0, 0.0)
    for b in range(mirror_ref.shape[0]):
        rev = (jnp.dot(anti_ref[...], mirror_ref[b], preferred_element_type=F32)
               + (1.0 - first) * jnp.dot(pick_ref[...], above_ref[b], preferred_element_type=F32)
               + first * jnp.dot(pick_ref[...], nyq_ref[b], preferred_element_type=F32))
        rev = rev.astype(BF16)
        for g in range(rev.shape[1] // group):
            cols = slice(g * group, (g + 1) * group)
            o_ref[b, :, cols] = jnp.dot(rev[:, cols], perm_ref[...],
                                        preferred_element_type=F32).astype(BF16)


def _unfold(f_lo, nyq, *, group_dim, t):
    batch, half_seq, w = f_lo.shape
    nb2 = half_seq // t
    sub = nyq.shape[1]
    r = jnp.arange(t, dtype=jnp.int32)
    anti = (r[:, None] + r[None, :] == t).astype(BF16)
    pick = (r[:, None] + jnp.arange(sub, dtype=jnp.int32)[None, :] == 0).astype(BF16)
    c = jnp.arange(group_dim, dtype=jnp.int32)
    perm = ((c[:, None] + c[None, :]) % group_dim == 0).astype(BF16)
    last_sub = half_seq // sub - 1
    return pl.pallas_call(
        _unfold_kernel,
        grid=(nb2,),
        in_specs=[
            pl.BlockSpec((t, t), lambda i: (0, 0)),
            pl.BlockSpec((t, sub), lambda i: (0, 0)),
            pl.BlockSpec((group_dim, group_dim), lambda i: (0, 0)),
            pl.BlockSpec((batch, t, w), lambda i: (0, nb2 - 1 - i, 0)),
            pl.BlockSpec((batch, sub, w),
                         lambda i: (0, jnp.minimum((nb2 - i) * (t // sub), last_sub), 0)),
            pl.BlockSpec((batch, sub, w), lambda i: (0, 0, 0)),
        ],
        out_specs=pl.BlockSpec((batch, t, w), lambda i: (0, i, 0)),
        out_shape=jax.ShapeDtypeStruct(f_lo.shape, f_lo.dtype),
        compiler_params=_params("parallel"),
        name="unfold",
    )(anti, pick, perm, f_lo, f_lo, nyq)


def _attn_kernel(bounded_ref, qT_ref, k_ref, vT_ref, lq1_ref, lk1_ref, lq2_ref, lk2_ref, sub_ref,
                 o_ref, m_ref, l_ref, acc_ref, *, tk, lam_init):
    qT = qT_ref[0]
    qk_dim = qT.shape[0] // 2
    tq = qT.shape[1]
    row = lax.broadcasted_iota(jnp.int32, qT.shape, 0)
    zero = jnp.zeros_like(qT)
    q2 = jnp.concatenate([jnp.where(row < qk_dim, qT, zero), jnp.where(row >= qk_dim, qT, zero)],
                         axis=1)
    n_kb = k_ref.shape[1] // tk

    l_ref[...] = jnp.zeros_like(l_ref)
    acc_ref[...] = jnp.zeros_like(acc_ref)

    def blocks(j):
        start = pl.multiple_of(j * tk, tk)
        return k_ref[0, pl.ds(start, tk), :], vT_ref[0, :, pl.ds(start, tk)]

    @pl.when(bounded_ref[0] == 1)
    def _():
        def body(j, carry):
            kb, vb = blocks(j)
            s = jnp.dot(kb, q2, preferred_element_type=F32)
            p = jnp.exp2(s)
            l_ref[...] += jnp.sum(p.reshape(tk // 8, 8, 2 * tq), axis=0)
            acc_ref[...] += jnp.dot(vb, p.astype(BF16), preferred_element_type=F32)
            return carry

        lax.fori_loop(0, n_kb, body, 0)

    @pl.when(bounded_ref[0] != 1)
    def _():
        m_ref[...] = jnp.full_like(m_ref, NEG_BIG)

        def body(j, carry):
            kb, vb = blocks(j)
            s = jnp.dot(kb, q2, preferred_element_type=F32)
            m_prev = m_ref[...]
            m_new = jnp.maximum(m_prev, jnp.max(s, axis=0, keepdims=True))
            alpha = jnp.exp2(m_prev - m_new)
            p = jnp.exp2(s - m_new)
            l_ref[...] = alpha * l_ref[...] + jnp.sum(p.reshape(tk // 8, 8, 2 * tq), axis=0)
            acc_ref[...] = acc_ref[...] * alpha + jnp.dot(vb, p.astype(BF16),
                                                          preferred_element_type=F32)
            m_ref[...] = m_new
            return carry

        lax.fori_loop(0, n_kb, body, 0)

    lam = (jnp.exp(jnp.sum(lq1_ref[...] * lk1_ref[...], axis=-1, keepdims=True))
           - jnp.exp(jnp.sum(lq2_ref[...] * lk2_ref[...], axis=-1, keepdims=True)) + lam_init)
    l = jnp.sum(l_ref[...], axis=0, keepdims=True)
    o = acc_ref[:, :tq] * (1.0 / l[:, :tq]) - acc_ref[:, tq:] * (lam / l[:, tq:])
    scale = lax.rsqrt(jnp.mean(o * o, axis=0, keepdims=True) + EPS)
    y = o * scale * (sub_ref[...] * (1.0 - lam_init))
    o_ref[0] = y.T.astype(o_ref.dtype)


def _attention(bounded, qT, k, vT, lq1, lk1, lq2, lk2, sub, *, lam_init, n_heads, tq, tk):
    batch, d_qk, seq = qT.shape
    d_v = vT.shape[1]
    hq, hv = d_qk // n_heads, d_v // n_heads
    vec = lambda b, h, i, flag: (0, 0)
    return pl.pallas_call(
        functools.partial(_attn_kernel, tk=tk, lam_init=lam_init),
        grid_spec=pltpu.PrefetchScalarGridSpec(
            num_scalar_prefetch=1,
            grid=(batch, n_heads, seq // tq),
            in_specs=[
                pl.BlockSpec((1, hq, tq), lambda b, h, i, flag: (b, h, i)),
                pl.BlockSpec((1, seq, hq), lambda b, h, i, flag: (b, 0, h)),
                pl.BlockSpec((1, hv, seq), lambda b, h, i, flag: (b, h, 0)),
                pl.BlockSpec(lq1.shape, vec),
                pl.BlockSpec(lk1.shape, vec),
                pl.BlockSpec(lq2.shape, vec),
                pl.BlockSpec(lk2.shape, vec),
                pl.BlockSpec(sub.shape, vec),
            ],
            out_specs=pl.BlockSpec((1, tq, hv), lambda b, h, i, flag: (b, i, h)),
            scratch_shapes=[
                pltpu.VMEM((1, 2 * tq), F32),
                pltpu.VMEM((8, 2 * tq), F32),
                pltpu.VMEM((hv, 2 * tq), F32),
            ],
        ),
        out_shape=jax.ShapeDtypeStruct((batch, seq, d_v), BF16),
        compiler_params=_params("parallel", "parallel", "arbitrary"),
        name="attn",
    )(bounded, qT, k, vT, lq1, lk1, lq2, lk2, sub)


def _merge_kernel(x_ref, flo_ref, fhi_ref, o_ref, sg_ref, pf_ref, pa_ref, wo_ref, out_ref, *,
                  tiles_per_seq):
    d = x_ref.shape[1]
    upper = pl.program_id(0) % tiles_per_seq >= tiles_per_seq // 2
    f = jnp.where(upper, fhi_ref[0], flo_ref[0])
    bf = jnp.dot(f, pf_ref[...], preferred_element_type=F32)
    ba = jnp.dot(o_ref[...], pa_ref[...], preferred_element_type=F32)
    mix = (jax.nn.sigmoid(sg_ref[:, :d].astype(F32)) * bf
           + jax.nn.sigmoid(sg_ref[:, d:].astype(F32)) * ba)
    out_ref[...] = x_ref[...] + jnp.dot(mix.astype(BF16), wo_ref[...], preferred_element_type=F32)


def _merge(x, f_lo, f_hi, o, sg, p_f, p_a, w_o, *, layer, tm):
    m, d = x.shape
    half_tiles = f_lo.shape[1] // tm
    n_s = 2 * half_tiles
    w = f_lo.shape[2]
    row = lambda i: (i, 0)
    stacked = lambda w: pl.BlockSpec((None,) + w.shape[1:], lambda i: (layer, 0, 0),
                                     pipeline_mode=pl.Buffered(1))
    return pl.pallas_call(
        functools.partial(_merge_kernel, tiles_per_seq=n_s),
        grid=(m // tm,),
        in_specs=[
            pl.BlockSpec((tm, d), row),
            pl.BlockSpec((1, tm, w), lambda i: (i // n_s, jnp.minimum(i % n_s, half_tiles - 1), 0)),
            pl.BlockSpec((1, tm, w), lambda i: (i // n_s, jnp.maximum(i % n_s - half_tiles, 0), 0)),
            pl.BlockSpec((tm, o.shape[1]), row),
            pl.BlockSpec((tm, sg.shape[1]), row),
            stacked(p_f),
            stacked(p_a),
            stacked(w_o),
        ],
        out_specs=pl.BlockSpec((tm, d), row),
        out_shape=jax.ShapeDtypeStruct((m, d), F32),
        compiler_params=_params("parallel"),
        name="merge",
    )(x, f_lo, f_hi, o, sg, p_f, p_a, w_o)


def _dft_tables(seq, group_dim):
    s = jnp.arange(seq // 2, dtype=jnp.int32)[None, :]
    hi = jnp.arange(seq // V7X_LANES, dtype=jnp.int32)[:, None]
    lo = jnp.arange(V7X_LANES, dtype=jnp.int32)[:, None]
    ang_hi = ((hi * V7X_LANES * s) % seq).astype(F32) * (2.0 * math.pi / seq)
    ang_lo = ((lo * s) % seq).astype(F32) * (2.0 * math.pi / seq)
    ca, sa = jnp.cos(ang_hi)[:, None, :], jnp.sin(ang_hi)[:, None, :]
    cb, sb = jnp.cos(ang_lo)[None, :, :], jnp.sin(ang_lo)[None, :, :]
    k_par = (1 - 2 * (jnp.arange(seq, dtype=jnp.int32) % 2)).astype(F32)[:, None]
    col0 = s == 0
    cmat = jnp.where(col0, 0.5, (ca * cb - sa * sb).reshape(seq, seq // 2)).astype(BF16)
    smat = jnp.where(col0, k_par, (-(sa * cb + ca * sb)).reshape(seq, seq // 2)).astype(BF16)
    c = jnp.arange(group_dim, dtype=jnp.int32)
    angc = ((c[:, None] * c[None, :]) % group_dim).astype(F32) * (2.0 * math.pi / group_dim)
    ortho = 1.0 / math.sqrt(seq * group_dim)
    wc = (jnp.concatenate([jnp.cos(angc), jnp.sin(angc)], axis=1) * ortho).astype(BF16)
    return cmat, smat, wc


def _rope_lane_tables(seq, head_dim):
    rope_dim = head_dim // 4
    half = rope_dim // 2
    pos = jnp.arange(seq, dtype=F32)
    inv_freq = ROPE_THETA ** (-jnp.arange(0, rope_dim, 2, dtype=F32) / rope_dim)
    ang = pos[:, None] * inv_freq[None, :]
    cos, sin = jnp.cos(ang), jnp.sin(ang)
    ones = jnp.ones((seq, head_dim - rope_dim), F32)
    zeros_h = jnp.zeros((seq, half), F32)
    zeros_r = jnp.zeros((seq, head_dim - rope_dim), F32)
    rc = jnp.concatenate([cos, cos, ones], axis=1)
    rs1 = jnp.concatenate([-sin, zeros_h, zeros_r], axis=1)
    rs2 = jnp.concatenate([zeros_h, sin, zeros_r], axis=1)
    reps = V7X_LANES // head_dim
    return tuple(jnp.tile(t, (1, reps)) for t in (rc, rs1, rs2))


def kernel(x, norm_ffa, ffa_gate, ffa_up, ffa_down, norm_mix, w_in, q_norm, k_norm, lambda_q1,
           lambda_k1, lambda_q2, lambda_k2, subln, p_f, p_a, w_o, norm_ffb, ffb_gate, ffb_up,
           ffb_down, norm_out):
    batch, seq, d = x.shape
    depth = w_in.shape[0]
    head_dim = q_norm.shape[1]
    v_dim = subln.shape[1]
    d_f = p_f.shape[1]
    d_v = p_a.shape[1]
    n_heads = d_v // v_dim
    group_dim = d_f // N_FOURIER_GROUPS
    tn = d_f
    assert w_in.shape[2] == 4 * tn + 2 * d and d_v == tn and n_heads * 2 * head_dim == tn
    assert 2 * head_dim == V7X_LANES and v_dim == V7X_LANES

    m = batch * seq
    tm_ffn_a = min(1024, seq)
    tm_ffn_b = min(1024, seq)
    tm_in = min(1024, seq)
    tm_merge = min(512, seq // 2)
    t_dft = min(1024, seq)
    t_fold = min(512, seq // 2)
    tq = min(2048, seq)
    tk = min(2048, seq)
    tf = 512

    cmat, smat, wc = _dft_tables(seq, group_dim)
    rc, rs1, rs2 = _rope_lane_tables(seq, head_dim)
    blk = jnp.arange(V7X_MXU_DIM, dtype=jnp.int32) // head_dim
    e = (blk[:, None] == blk[None, :]).astype(BF16)
    q_scale = head_dim ** -0.5 * LOG2E

    ffa_gate, ffa_up, ffa_down, w_in, p_f, p_a, w_o, ffb_gate, ffb_up, ffb_down = (
        w.astype(BF16) for w in (ffa_gate, ffa_up, ffa_down, w_in, p_f, p_a, w_o, ffb_gate, ffb_up,
                                 ffb_down))
    xs = x.reshape(m, d)
    for i in range(depth):
        lam_init = 0.8 - 0.6 * math.exp(-0.3 * i)
        xs, h = _ffn(xs, norm_ffa[i][None], ffa_gate, ffa_up, ffa_down, norm_mix[i][None], layer=i,
                     tail="norm_copy", tm=tm_ffn_a, tf=tf)
        qg = jnp.tile(q_norm[i], tn // head_dim)[None] * q_scale
        kg = jnp.tile(k_norm[i], tn // head_dim)[None]
        project = functools.partial(_project, h, w_in, layer=i, batch=batch, seq=seq, tm=tm_in,
                                    tn=tn, group_dim=group_dim, head_dim=head_dim)
        ab = project((wc,), kind="fourier", first_block=0, n_col_blocks=1)
        qT = project((e, qg, rc, rs1, rs2), kind="q", first_block=1, n_col_blocks=1)
        k = project((e, kg, rc, rs1, rs2), kind="k", first_block=2, n_col_blocks=1)
        vT = project((), kind="v", first_block=3, n_col_blocks=1)
        sg = project((), kind="gate", first_block=4, n_col_blocks=2 * d // tn)
        folded = _fold(ab.reshape(batch, seq, 2 * tn), t=t_fold)
        t_half = min(t_dft, seq // 2)
        f_lo = _fourier(cmat, smat, folded, first_row=0, n_rows=seq // 2, out_rows=seq // 2,
                        tm=t_half, tk=t_half)
        nyq = _fourier(cmat, smat, folded, first_row=seq // 2, n_rows=NYQUIST_ROWS,
                       out_rows=NYQUIST_ROWS, tm=NYQUIST_ROWS, tk=t_half)
        f_hi = _unfold(f_lo, nyq, group_dim=group_dim, t=t_fold)
        score_bound = head_dim * q_scale * jnp.max(jnp.abs(q_norm[i])) * jnp.max(jnp.abs(k_norm[i]))
        bounded = (score_bound <= SCORE_BOUND_LIMIT).astype(jnp.int32).reshape(1)
        o = _attention(bounded, qT, k.reshape(batch, seq, tn), vT, lambda_q1[i][None],
                       lambda_k1[i][None], lambda_q2[i][None], lambda_k2[i][None],
                       subln[i][:, None], lam_init=lam_init, n_heads=n_heads, tq=tq, tk=tk)
        xs = _merge(xs, f_lo, f_hi, o.reshape(m, d_v), sg, p_f, p_a, w_o, layer=i, tm=tm_merge)
        xs = _ffn(xs, norm_ffb[i][None], ffb_gate, ffb_up, ffb_down, norm_out[i][None], layer=i,
                  tail="norm_inplace", tm=tm_ffn_b, tf=tf)
    return xs.reshape(batch, seq, d)
```

```python
import functools
import math

import jax
import jax.numpy as jnp
from jax import lax
from jax.experimental import pallas as pl
from jax.experimental.pallas import tpu as pltpu

F32 = jnp.float32
BF16 = jnp.bfloat16

N_FOURIER_GROUPS = 4
ROPE_THETA = 500000.0
EPS = 1e-6
LOG2E = 1.4426950408889634
NEG_BIG = -1e30
SCORE_BOUND_LIMIT = 64.0

NYQUIST_ROWS = 16

V7X_LANES = 128
V7X_MXU_DIM = 256
V7X_VMEM_LIMIT_BYTES = 60 * 1024 * 1024


def _params(*semantics):
    return pltpu.CompilerParams(dimension_semantics=semantics,
                                vmem_limit_bytes=V7X_VMEM_LIMIT_BYTES)


def _rms_scale(x):
    return lax.rsqrt(jnp.mean(x * x, axis=-1, keepdims=True) + EPS)


def _ffn_kernel(x_ref, gin_ref, wg_ref, wu_ref, wd_ref, gtail_ref, o_ref, h_ref, *, tail):
    j = pl.program_id(1)

    def chunk(h):
        g = jnp.dot(h, wg_ref[...], preferred_element_type=F32)
        u = jnp.dot(h, wu_ref[...], preferred_element_type=F32)
        a = (0.5 * g) * jax.nn.sigmoid(g) * u
        return jnp.dot(a.astype(BF16), wd_ref[...], preferred_element_type=F32)

    @pl.when(j == 0)
    def _():
        x = x_ref[...]
        h = (x * _rms_scale(x) * gin_ref[...]).astype(BF16)
        h_ref[...] = h
        o_ref[...] = x + chunk(h)

    last = pl.num_programs(1) - 1

    @pl.when(jnp.logical_and(j > 0, j < last))
    def _():
        o_ref[...] += chunk(h_ref[...])

    @pl.when(j == last)
    def _():
        half = o_ref.shape[0] // 2
        for r in range(2):
            rows = slice(r * half, (r + 1) * half)
            y = o_ref[rows, :] + chunk(h_ref[rows, :])
            yn = y * _rms_scale(y) * gtail_ref[...]
            if tail == "norm_inplace":
                o_ref[rows, :] = yn
            else:
                o_ref[rows, :] = y
                h_ref[rows, :] = yn.astype(BF16)


def _ffn(x, g_in, wg, wu, wd, g_tail, *, layer, tail, tm, tf):
    m, d = x.shape
    n_chunks = wg.shape[2] // tf
    tile = pl.BlockSpec((tm, d), lambda i, j: (i, 0))
    copy = tail == "norm_copy"
    return pl.pallas_call(
        functools.partial(_ffn_kernel, tail=tail),
        grid=(m // tm, n_chunks),
        in_specs=[
            tile,
            pl.BlockSpec((1, d), lambda i, j: (0, 0)),
            pl.BlockSpec((None, d, tf), lambda i, j: (layer, 0, j)),
            pl.BlockSpec((None, d, tf), lambda i, j: (layer, 0, j)),
            pl.BlockSpec((None, tf, d), lambda i, j: (layer, j, 0)),
            pl.BlockSpec((1, d), lambda i, j: (0, 0)),
        ],
        out_specs=[tile, tile] if copy else tile,
        out_shape=([jax.ShapeDtypeStruct((m, d), F32), jax.ShapeDtypeStruct((m, d), BF16)] if copy
                   else jax.ShapeDtypeStruct((m, d), F32)),
        scratch_shapes=[] if copy else [pltpu.VMEM((tm, d), BF16)],
        compiler_params=_params("parallel", "arbitrary"),
        name="ffn",
    )(x, g_in, wg, wu, wd, g_tail)


def _norm_rope(z, e_ref, gain, rc, rs1, rs2, head_dim):
    tn = z.shape[1]
    zz = (z * z).astype(BF16)
    half = head_dim // 8
    outs = []
    for c in range(tn // V7X_MXU_DIM):
        cols = slice(c * V7X_MXU_DIM, (c + 1) * V7X_MXU_DIM)
        ss = jnp.dot(zz[:, cols], e_ref[...], preferred_element_type=F32)
        y = z[:, cols] * lax.rsqrt(ss * (1.0 / head_dim) + EPS) * gain[:, cols]
        for hh in range(V7X_MXU_DIM // V7X_LANES):
            yc = y[:, hh * V7X_LANES:(hh + 1) * V7X_LANES]
            outs.append(yc * rc + pltpu.roll(yc, V7X_LANES - half, 1) * rs1
                        + pltpu.roll(yc, half, 1) * rs2)
    return jnp.concatenate(outs, axis=1)


def _proj_kernel(*refs, kind, n_tiles, group_dim, head_dim):
    h_ref, w_ref = refs[:2]
    aux, out_ref, z_ref = refs[2:-2], refs[-2], refs[-1]
    t = pl.program_id(1)

    def project(slot):
        z_ref[slot] = jnp.dot(h_ref[...], w_ref[...], preferred_element_type=F32)

    def epilogue(z):
        if kind == "fourier":
            wc_ref, = aux
            u = z.astype(BF16)
            n_groups = z.shape[1] // group_dim
            for g in range(n_groups):
                c = jnp.dot(u[:, g * group_dim:(g + 1) * group_dim], wc_ref[...],
                            preferred_element_type=F32)
                out_ref[:, g * group_dim:(g + 1) * group_dim] = c[:, :group_dim].astype(BF16)
                out_ref[:, (n_groups + g) * group_dim:(n_groups + g + 1) * group_dim] = (
                    c[:, group_dim:].astype(BF16))
        elif kind in ("q", "k"):
            e_ref, gain_ref, rc_ref, rs1_ref, rs2_ref = aux
            y = _norm_rope(z, e_ref, gain_ref[...], rc_ref[...], rs1_ref[...], rs2_ref[...],
                           head_dim)
            if kind == "q":
                out_ref[0] = y.T.astype(BF16)
            else:
                out_ref[...] = y.astype(BF16)
        elif kind == "v":
            out_ref[0] = z.T.astype(BF16)
        else:
            out_ref[...] = z.astype(BF16)

    @pl.when(t == 0)
    def _():
        project(0)

    for parity in range(2):
        @pl.when(jnp.logical_and(jnp.logical_and(t > 0, t < n_tiles), t % 2 == parity))
        def _(parity=parity):
            epilogue(z_ref[1 - parity])
            project(parity)

    @pl.when(t == n_tiles)
    def _():
        epilogue(z_ref[(n_tiles - 1) % 2])


def _project(h, w_in, aux, *, kind, layer, first_block, n_col_blocks, batch, seq, tm, tn, group_dim,
             head_dim):
    m, d = h.shape
    n_tiles, n_s = m // tm, seq // tm
    done = lambda t: jnp.maximum(t - 1, 0)
    const = lambda c, t: (0, 0)
    pos = lambda c, t: (done(t) % n_s, 0)
    aux_specs = {
        "fourier": [pl.BlockSpec(a.shape, const) for a in aux],
        "q": [pl.BlockSpec(a.shape, const) for a in aux[:2]]
             + [pl.BlockSpec((tm, V7X_LANES), pos) for _ in aux[2:]],
        "v": [],
        "gate": [],
    }
    aux_specs["k"] = aux_specs["q"]
    if kind == "fourier":
        out_shape, out_spec = (m, 2 * tn), pl.BlockSpec((tm, 2 * tn), lambda c, t: (done(t), 0))
    elif kind in ("q", "v"):
        out_shape = (batch, tn, seq)
        out_spec = pl.BlockSpec((1, tn, tm), lambda c, t: (done(t) // n_s, 0, done(t) % n_s))
    else:
        out_shape = (m, n_col_blocks * tn)
        out_spec = pl.BlockSpec((tm, tn), lambda c, t: (done(t), c))
    return pl.pallas_call(
        functools.partial(_proj_kernel, kind=kind, n_tiles=n_tiles, group_dim=group_dim,
                          head_dim=head_dim),
        grid=(n_col_blocks, n_tiles + 1),
        in_specs=[
            pl.BlockSpec((tm, d), lambda c, t: (jnp.minimum(t, n_tiles - 1), 0)),
            pl.BlockSpec((None, d, tn), lambda c, t: (layer, 0, first_block + c)),
        ] + aux_specs[kind],
        out_specs=out_spec,
        out_shape=jax.ShapeDtypeStruct(out_shape, BF16),
        scratch_shapes=[pltpu.VMEM((2, tm, tn), F32)],
        compiler_params=_params("arbitrary", "arbitrary"),
        name="proj_" + kind,
    )(h, w_in, *aux)


def _fold_kernel(anti_ref, pick_ref, lo_ref, hi_ref, above_ref, mid_ref, o_ref):
    w = lo_ref.shape[2] // 2
    first = jnp.where(pl.program_id(0) == 0, 1.0, 0.0)
    for b in range(lo_ref.shape[0]):
        rev = (jnp.dot(anti_ref[...], hi_ref[b], preferred_element_type=F32)
               + jnp.dot(pick_ref[...], above_ref[b], preferred_element_type=F32))
        mid = jnp.dot(pick_ref[...], mid_ref[b], preferred_element_type=F32)
        lo = lo_ref[b].astype(F32)
        o_ref[b, :, :w] = (lo[:, :w] + rev[:, :w]).astype(BF16)
        o_ref[b, :, w:] = (lo[:, w:] - rev[:, w:] + first * mid[:, :w]).astype(BF16)


def _fold(ab, *, t):
    batch, seq, two_w = ab.shape
    nb = seq // t
    sub = 16
    r = jnp.arange(t, dtype=jnp.int32)
    anti = (r[:, None] + r[None, :] == t).astype(BF16)
    pick = (r[:, None] + jnp.arange(sub, dtype=jnp.int32)[None, :] == 0).astype(BF16)
    return pl.pallas_call(
        _fold_kernel,
        grid=(nb // 2,),
        in_specs=[
            pl.BlockSpec((t, t), lambda i: (0, 0)),
            pl.BlockSpec((t, sub), lambda i: (0, 0)),
            pl.BlockSpec((batch, t, two_w), lambda i: (0, i, 0)),
            pl.BlockSpec((batch, t, two_w), lambda i: (0, nb - 1 - i, 0)),
            pl.BlockSpec((batch, sub, two_w), lambda i: (0, ((nb - i) * (t // sub)) % (seq // sub), 0)),
            pl.BlockSpec((batch, sub, two_w), lambda i: (0, seq // 2 // sub, 0)),
        ],
        out_specs=pl.BlockSpec((batch, t, two_w), lambda i: (0, i, 0)),
        out_shape=jax.ShapeDtypeStruct((batch, seq // 2, two_w), BF16),
        compiler_params=_params("parallel"),
        name="fold",
    )(anti, pick, ab, ab, ab, ab)


def _fourier_kernel(c_ref, s_ref, a_ref, b_ref, o_ref, acc_ref):
    kk = pl.program_id(1)

    @pl.when(kk == 0)
    def _():
        acc_ref[...] = jnp.zeros_like(acc_ref)

    for b in range(a_ref.shape[0]):
        acc_ref[b] += (jnp.dot(c_ref[...], a_ref[b], preferred_element_type=F32)
                       + jnp.dot(s_ref[...], b_ref[b], preferred_element_type=F32))

    @pl.when(kk == pl.num_programs(1) - 1)
    def _():
        o_ref[...] = acc_ref[...].astype(o_ref.dtype)


def _fourier(cmat, smat, folded, *, first_row, n_rows, out_rows, tm, tk):
    batch, half_seq, two_w = folded.shape
    w = two_w // 2
    row0 = first_row // tm
    return pl.pallas_call(
        _fourier_kernel,
        grid=(n_rows // tm, half_seq // tk),
        in_specs=[
            pl.BlockSpec((tm, tk), lambda i, kk: (row0 + i, kk)),
            pl.BlockSpec((tm, tk), lambda i, kk: (row0 + i, kk)),
            pl.BlockSpec((batch, tk, w), lambda i, kk: (0, kk, 0)),
            pl.BlockSpec((batch, tk, w), lambda i, kk: (0, kk, 1)),
        ],
        out_specs=pl.BlockSpec((batch, tm, w), lambda i, kk: (0, i, 0)),
        out_shape=jax.ShapeDtypeStruct((batch, out_rows, w), BF16),
        scratch_shapes=[pltpu.VMEM((batch, tm, w), F32)],
        compiler_params=_params("parallel", "arbitrary"),
        name="fourier",
    )(cmat, smat, folded, folded)


def _unfold_kernel(anti_ref, pick_ref, perm_ref, mirror_ref, above_ref, nyq_ref, o_ref):
    group = perm_ref.shape[0]
    first = jnp.where(pl.program_id(0) == 0, 1.0, 0.0)
    for b in range(mirror_ref.shape[0]):
        rev = (jnp.dot(anti_ref[...], mirror_ref[b], preferred_element_type=F32)
               + (1.0 - first) * jnp.dot(pick_ref[...], above_ref[b], preferred_element_type=F32)
               + first * jnp.dot(pick_ref[...], nyq_ref[b], preferred_element_type=F32))
        rev = rev.astype(BF16)
        for g in range(rev.shape[1] // group):
            cols = slice(g * group, (g + 1) * group)
            o_ref[b, :, cols] = jnp.dot(rev[:, cols], perm_ref[...],
                                        preferred_element_type=F32).astype(BF16)


def _unfold(f_lo, nyq, *, group_dim, t):
    batch, half_seq, w = f_lo.shape
    nb2 = half_seq // t
    sub = nyq.shape[1]
    r = jnp.arange(t, dtype=jnp.int32)
    anti = (r[:, None] + r[None, :] == t).astype(BF16)
    pick = (r[:, None] + jnp.arange(sub, dtype=jnp.int32)[None, :] == 0).astype(BF16)
    c = jnp.arange(group_dim, dtype=jnp.int32)
    perm = ((c[:, None] + c[None, :]) % group_dim == 0).astype(BF16)
    last_sub = half_seq // sub - 1
    return pl.pallas_call(
        _unfold_kernel,
        grid=(nb2,),
        in_specs=[
            pl.BlockSpec((t, t), lambda i: (0, 0)),
            pl.BlockSpec((t, sub), lambda i: (0, 0)),
            pl.BlockSpec((group_dim, group_dim), lambda i: (0, 0)),
            pl.BlockSpec((batch, t, w), lambda i: (0, nb2 - 1 - i, 0)),
            pl.BlockSpec((batch, sub, w),
                         lambda i: (0, jnp.minimum((nb2 - i) * (t // sub), last_sub), 0)),
            pl.BlockSpec((batch, sub, w), lambda i: (0, 0, 0)),
        ],
        out_specs=pl.BlockSpec((batch, t, w), lambda i: (0, i, 0)),
        out_shape=jax.ShapeDtypeStruct(f_lo.shape, f_lo.dtype),
        compiler_params=_params("parallel"),
        name="unfold",
    )(anti, pick, perm, f_lo, f_lo, nyq)


def _attn_kernel(bounded_ref, qT_ref, k_ref, vT_ref, lq1_ref, lk1_ref, lq2_ref, lk2_ref, sub_ref,
                 o_ref, m_ref, l_ref, acc_ref, *, tk, lam_init):
    qT = qT_ref[0]
    qk_dim = qT.shape[0] // 2
    tq = qT.shape[1]
    row = lax.broadcasted_iota(jnp.int32, qT.shape, 0)
    zero = jnp.zeros_like(qT)
    q2 = jnp.concatenate([jnp.where(row < qk_dim, qT, zero), jnp.where(row >= qk_dim, qT, zero)],
                         axis=1)
    n_kb = k_ref.shape[1] // tk

    l_ref[...] = jnp.zeros_like(l_ref)
    acc_ref[...] = jnp.zeros_like(acc_ref)

    def blocks(j):
        start = pl.multiple_of(j * tk, tk)
        return k_ref[0, pl.ds(start, tk), :], vT_ref[0, :, pl.ds(start, tk)]

    @pl.when(bounded_ref[0] == 1)
    def _():
        def body(j, carry):
            kb, vb = blocks(j)
            s = jnp.dot(kb, q2, preferred_element_type=F32)
            p = jnp.exp2(s)
            l_ref[...] += jnp.sum(p.reshape(tk // 8, 8, 2 * tq), axis=0)
            acc_ref[...] += jnp.dot(vb, p.astype(BF16), preferred_element_type=F32)
            return carry

        lax.fori_loop(0, n_kb, body, 0)

    @pl.when(bounded_ref[0] != 1)
    def _():
        m_ref[...] = jnp.full_like(m_ref, NEG_BIG)

        def body(j, carry):
            kb, vb = blocks(j)
            s = jnp.dot(kb, q2, preferred_element_type=F32)
            m_prev = m_ref[...]
            m_new = jnp.maximum(m_prev, jnp.max(s, axis=0, keepdims=True))
            alpha = jnp.exp2(m_prev - m_new)
            p = jnp.exp2(s - m_new)
            l_ref[...] = alpha * l_ref[...] + jnp.sum(p.reshape(tk // 8, 8, 2 * tq), axis=0)
            acc_ref[...] = acc_ref[...] * alpha + jnp.dot(vb, p.astype(BF16),
                                                          preferred_element_type=F32)
            m_ref[...] = m_new
            return carry

        lax.fori_loop(0, n_kb, body, 0)

    lam = (jnp.exp(jnp.sum(lq1_ref[...] * lk1_ref[...], axis=-1, keepdims=True))
           - jnp.exp(jnp.sum(lq2_ref[...] * lk2_ref[...], axis=-1, keepdims=True)) + lam_init)
    l = jnp.sum(l_ref[...], axis=0, keepdims=True)
    o = acc_ref[:, :tq] * (1.0 / l[:, :tq]) - acc_ref[:, tq:] * (lam / l[:, tq:])
    scale = lax.rsqrt(jnp.mean(o * o, axis=0, keepdims=True) + EPS)
    y = o * scale * (sub_ref[...] * (1.0 - lam_init))
    o_ref[0] = y.T.astype(o_ref.dtype)


def _attention(bounded, qT, k, vT, lq1, lk1, lq2, lk2, sub, *, lam_init, n_heads, tq, tk):
    batch, d_qk, seq = qT.shape
    d_v = vT.shape[1]
    hq, hv = d_qk // n_heads, d_v // n_heads
    vec = lambda b, h, i, flag: (0, 0)
    return pl.pallas_call(
        functools.partial(_attn_kernel, tk=tk, lam_init=lam_init),
        grid_spec=pltpu.PrefetchScalarGridSpec(
            num_scalar_prefetch=1,
            grid=(batch, n_heads, seq // tq),
            in_specs=[
                pl.BlockSpec((1, hq, tq), lambda b, h, i, flag: (b, h, i)),
                pl.BlockSpec((1, seq, hq), lambda b, h, i, flag: (b, 0, h)),
                pl.BlockSpec((1, hv, seq), lambda b, h, i, flag: (b, h, 0)),
                pl.BlockSpec(lq1.shape, vec),
                pl.BlockSpec(lk1.shape, vec),
                pl.BlockSpec(lq2.shape, vec),
                pl.BlockSpec(lk2.shape, vec),
                pl.BlockSpec(sub.shape, vec),
            ],
            out_specs=pl.BlockSpec((1, tq, hv), lambda b, h, i, flag: (b, i, h)),
            scratch_shapes=[
                pltpu.VMEM((1, 2 * tq), F32),
                pltpu.VMEM((8, 2 * tq), F32),
                pltpu.VMEM((hv, 2 * tq), F32),
            ],
        ),
        out_shape=jax.ShapeDtypeStruct((batch, seq, d_v), BF16),
        compiler_params=_params("parallel", "parallel", "arbitrary"),
        name="attn",
    )(bounded, qT, k, vT, lq1, lk1, lq2, lk2, sub)


def _merge_kernel(x_ref, flo_ref, fhi_ref, o_ref, sg_ref, pf_ref, pa_ref, wo_ref, out_ref, *,
                  tiles_per_seq):
    d = x_ref.shape[1]
    upper = pl.program_id(0) % tiles_per_seq >= tiles_per_seq // 2
    f = jnp.where(upper, fhi_ref[0], flo_ref[0])
    bf = jnp.dot(f, pf_ref[...], preferred_element_type=F32)
    ba = jnp.dot(o_ref[...], pa_ref[...], preferred_element_type=F32)
    mix = (jax.nn.sigmoid(sg_ref[:, :d].astype(F32)) * bf
           + jax.nn.sigmoid(sg_ref[:, d:].astype(F32)) * ba)
    out_ref[...] = x_ref[...] + jnp.dot(mix.astype(BF16), wo_ref[...], preferred_element_type=F32)


def _merge(x, f_lo, f_hi, o, sg, p_f, p_a, w_o, *, layer, tm):
    m, d = x.shape
    half_tiles = f_lo.shape[1] // tm
    n_s = 2 * half_tiles
    w = f_lo.shape[2]
    row = lambda i: (i, 0)
    stacked = lambda w: pl.BlockSpec((None,) + w.shape[1:], lambda i: (layer, 0, 0),
                                     pipeline_mode=pl.Buffered(1))
    return pl.pallas_call(
        functools.partial(_merge_kernel, tiles_per_seq=n_s),
        grid=(m // tm,),
        in_specs=[
            pl.BlockSpec((tm, d), row),
            pl.BlockSpec((1, tm, w), lambda i: (i // n_s, jnp.minimum(i % n_s, half_tiles - 1), 0)),
            pl.BlockSpec((1, tm, w), lambda i: (i // n_s, jnp.maximum(i % n_s - half_tiles, 0), 0)),
            pl.BlockSpec((tm, o.shape[1]), row),
            pl.BlockSpec((tm, sg.shape[1]), row),
            stacked(p_f),
            stacked(p_a),
            stacked(w_o),
        ],
        out_specs=pl.BlockSpec((tm, d), row),
        out_shape=jax.ShapeDtypeStruct((m, d), F32),
        compiler_params=_params("parallel"),
        name="merge",
    )(x, f_lo, f_hi, o, sg, p_f, p_a, w_o)


def _dft_tables(seq, group_dim):
    n_rows = seq // 2 + V7X_LANES
    s = jnp.arange(seq // 2, dtype=jnp.int32)[None, :]
    hi = jnp.arange(n_rows // V7X_LANES, dtype=jnp.int32)[:, None]
    lo = jnp.arange(V7X_LANES, dtype=jnp.int32)[:, None]
    ang_hi = ((hi * V7X_LANES * s) % seq).astype(F32) * (2.0 * math.pi / seq)
    ang_lo = ((lo * s) % seq).astype(F32) * (2.0 * math.pi / seq)
    ca, sa = jnp.cos(ang_hi)[:, None, :], jnp.sin(ang_hi)[:, None, :]
    cb, sb = jnp.cos(ang_lo)[None, :, :], jnp.sin(ang_lo)[None, :, :]
    k_par = (1 - 2 * (jnp.arange(n_rows, dtype=jnp.int32) % 2)).astype(F32)[:, None]
    col0 = s == 0
    cmat = jnp.where(col0, 0.5, (ca * cb - sa * sb).reshape(n_rows, seq // 2)).astype(BF16)
    smat = jnp.where(col0, k_par, (-(sa * cb + ca * sb)).reshape(n_rows, seq // 2)).astype(BF16)
    c = jnp.arange(group_dim, dtype=jnp.int32)
    angc = ((c[:, None] * c[None, :]) % group_dim).astype(F32) * (2.0 * math.pi / group_dim)
    ortho = 1.0 / math.sqrt(seq * group_dim)
    wc = (jnp.concatenate([jnp.cos(angc), jnp.sin(angc)], axis=1) * ortho).astype(BF16)
    return cmat, smat, wc


def _rope_lane_tables(seq, head_dim):
    rope_dim = head_dim // 4
    half = rope_dim // 2
    pos = jnp.arange(seq, dtype=F32)
    inv_freq = ROPE_THETA ** (-jnp.arange(0, rope_dim, 2, dtype=F32) / rope_dim)
    ang = pos[:, None] * inv_freq[None, :]
    cos, sin = jnp.cos(ang), jnp.sin(ang)
    ones = jnp.ones((seq, head_dim - rope_dim), F32)
    zeros_h = jnp.zeros((seq, half), F32)
    zeros_r = jnp.zeros((seq, head_dim - rope_dim), F32)
    rc = jnp.concatenate([cos, cos, ones], axis=1)
    rs1 = jnp.concatenate([-sin, zeros_h, zeros_r], axis=1)
    rs2 = jnp.concatenate([zeros_h, sin, zeros_r], axis=1)
    reps = V7X_LANES // head_dim
    return tuple(jnp.tile(t, (1, reps)) for t in (rc, rs1, rs2))


def kernel(x, norm_ffa, ffa_gate, ffa_up, ffa_down, norm_mix, w_in, q_norm, k_norm, lambda_q1,
           lambda_k1, lambda_q2, lambda_k2, subln, p_f, p_a, w_o, norm_ffb, ffb_gate, ffb_up,
           ffb_down, norm_out):
    batch, seq, d = x.shape
    depth = w_in.shape[0]
    head_dim = q_norm.shape[1]
    v_dim = subln.shape[1]
    d_f = p_f.shape[1]
    d_v = p_a.shape[1]
    n_heads = d_v // v_dim
    group_dim = d_f // N_FOURIER_GROUPS
    tn = d_f
    assert w_in.shape[2] == 4 * tn + 2 * d and d_v == tn and n_heads * 2 * head_dim == tn
    assert 2 * head_dim == V7X_LANES and v_dim == V7X_LANES

    m = batch * seq
    tm_ffn_a = min(1024, seq)
    tm_ffn_b = min(1024, seq)
    tm_in = min(1024, seq)
    tm_merge = min(512, seq // 2)
    t_dft = min(1024, seq)
    t_fold = min(512, seq // 2)
    tq = min(2048, seq)
    tk = min(2048, seq)
    tf = 512

    cmat, smat, wc = _dft_tables(seq, group_dim)
    rc, rs1, rs2 = _rope_lane_tables(seq, head_dim)
    blk = jnp.arange(V7X_MXU_DIM, dtype=jnp.int32) // head_dim
    e = (blk[:, None] == blk[None, :]).astype(BF16)
    q_scale = head_dim ** -0.5 * LOG2E

    ffa_gate, ffa_up, ffa_down, w_in, p_f, p_a, w_o, ffb_gate, ffb_up, ffb_down = (
        w.astype(BF16) for w in (ffa_gate, ffa_up, ffa_down, w_in, p_f, p_a, w_o, ffb_gate, ffb_up,
                                 ffb_down))
    xs = x.reshape(m, d)
    for i in range(depth):
        lam_init = 0.8 - 0.6 * math.exp(-0.3 * i)
        xs, h = _ffn(xs, norm_ffa[i][None], ffa_gate, ffa_up, ffa_down, norm_mix[i][None], layer=i,
                     tail="norm_copy", tm=tm_ffn_a, tf=tf)
        qg = jnp.tile(q_norm[i], tn // head_dim)[None] * q_scale
        kg = jnp.tile(k_norm[i], tn // head_dim)[None]
        project = functools.partial(_project, h, w_in, layer=i, batch=batch, seq=seq, tm=tm_in,
                                    tn=tn, group_dim=group_dim, head_dim=head_dim)
        ab = project((wc,), kind="fourier", first_block=0, n_col_blocks=1)
        qT = project((e, qg, rc, rs1, rs2), kind="q", first_block=1, n_col_blocks=1)
        k = project((e, kg, rc, rs1, rs2), kind="k", first_block=2, n_col_blocks=1)
        vT = project((), kind="v", first_block=3, n_col_blocks=1)
        sg = project((), kind="gate", first_block=4, n_col_blocks=2 * d // tn)
        folded = _fold(ab.reshape(batch, seq, 2 * tn), t=t_fold)
        t_half = min(t_dft, seq // 2)
        f_lo = _fourier(cmat, smat, folded, first_row=0, n_rows=seq // 2, out_rows=seq // 2,
                        tm=t_half, tk=t_half)
        nyq = _fourier(cmat, smat, folded, first_row=seq // 2, n_rows=NYQUIST_ROWS,
                       out_rows=NYQUIST_ROWS, tm=NYQUIST_ROWS, tk=t_half)
        f_hi = _unfold(f_lo, nyq, group_dim=group_dim, t=t_fold)
        score_bound = head_dim * q_scale * jnp.max(jnp.abs(q_norm[i])) * jnp.max(jnp.abs(k_norm[i]))
        bounded = (score_bound <= SCORE_BOUND_LIMIT).astype(jnp.int32).reshape(1)
        o = _attention(bounded, qT, k.reshape(batch, seq, tn), vT, lambda_q1[i][None],
                       lambda_k1[i][None], lambda_q2[i][None], lambda_k2[i][None],
                       subln[i][:, None], lam_init=lam_init, n_heads=n_heads, tq=tq, tk=tk)
        xs = _merge(xs, f_lo, f_hi, o.reshape(m, d_v), sg, p_f, p_a, w_o, layer=i, tm=tm_merge)
        xs = _ffn(xs, norm_ffb[i][None], ffb_gate, ffb_up, ffb_down, norm_out[i][None], layer=i,
                  tail="norm_inplace", tm=tm_ffn_b, tf=tf)
    return xs.reshape(batch, seq, d)
```

```python
import functools
import math

import jax
import jax.numpy as jnp
from jax import lax
from jax.experimental import pallas as pl
from jax.experimental.pallas import tpu as pltpu

F32 = jnp.float32
BF16 = jnp.bfloat16

N_FOURIER_GROUPS = 4
ROPE_THETA = 500000.0
EPS = 1e-6
LOG2E = 1.4426950408889634
NEG_BIG = -1e30
SCORE_BOUND_LIMIT = 64.0

NYQUIST_ROWS = 16

V7X_LANES = 128
V7X_MXU_DIM = 256
V7X_VMEM_LIMIT_BYTES = 60 * 1024 * 1024


def _params(*semantics):
    return pltpu.CompilerParams(dimension_semantics=semantics,
                                vmem_limit_bytes=V7X_VMEM_LIMIT_BYTES)


def _rms_scale(x):
    return lax.rsqrt(jnp.mean(x * x, axis=-1, keepdims=True) + EPS)


def _ffn_kernel(x_ref, gin_ref, wg_ref, wu_ref, wd_ref, gtail_ref, o_ref, h_ref, *, tail):
    j = pl.program_id(1)

    def chunk(h):
        g = jnp.dot(h, wg_ref[...], preferred_element_type=F32)
        u = jnp.dot(h, wu_ref[...], preferred_element_type=F32)
        a = (0.5 * g) * jax.nn.sigmoid(g) * u
        return jnp.dot(a.astype(BF16), wd_ref[...], preferred_element_type=F32)

    @pl.when(j == 0)
    def _():
        x = x_ref[...]
        h = (x * _rms_scale(x) * gin_ref[...]).astype(BF16)
        h_ref[...] = h
        o_ref[...] = x + chunk(h)

    last = pl.num_programs(1) - 1

    @pl.when(jnp.logical_and(j > 0, j < last))
    def _():
        o_ref[...] += chunk(h_ref[...])

    @pl.when(j == last)
    def _():
        half = o_ref.shape[0] // 2
        for r in range(2):
            rows = slice(r * half, (r + 1) * half)
            y = o_ref[rows, :] + chunk(h_ref[rows, :])
            yn = y * _rms_scale(y) * gtail_ref[...]
            if tail == "norm_inplace":
                o_ref[rows, :] = yn
            else:
                o_ref[rows, :] = y
                h_ref[rows, :] = yn.astype(BF16)


def _ffn(x, g_in, wg, wu, wd, g_tail, *, layer, tail, tm, tf):
    m, d = x.shape
    n_chunks = wg.shape[2] // tf
    tile = pl.BlockSpec((tm, d), lambda i, j: (i, 0))
    copy = tail == "norm_copy"
    return pl.pallas_call(
        functools.partial(_ffn_kernel, tail=tail),
        grid=(m // tm, n_chunks),
        in_specs=[
            tile,
            pl.BlockSpec((1, d), lambda i, j: (0, 0)),
            pl.BlockSpec((None, d, tf), lambda i, j: (layer, 0, j)),
            pl.BlockSpec((None, d, tf), lambda i, j: (layer, 0, j)),
            pl.BlockSpec((None, tf, d), lambda i, j: (layer, j, 0)),
            pl.BlockSpec((1, d), lambda i, j: (0, 0)),
        ],
        out_specs=[tile, tile] if copy else tile,
        out_shape=([jax.ShapeDtypeStruct((m, d), F32), jax.ShapeDtypeStruct((m, d), BF16)] if copy
                   else jax.ShapeDtypeStruct((m, d), F32)),
        scratch_shapes=[] if copy else [pltpu.VMEM((tm, d), BF16)],
        compiler_params=_params("parallel", "arbitrary"),
        name="ffn",
    )(x, g_in, wg, wu, wd, g_tail)


def _norm_rope(z, e_ref, gain, rc, rs1, rs2, head_dim):
    tn = z.shape[1]
    zz = (z * z).astype(BF16)
    half = head_dim // 8
    outs = []
    for c in range(tn // V7X_MXU_DIM):
        cols = slice(c * V7X_MXU_DIM, (c + 1) * V7X_MXU_DIM)
        ss = jnp.dot(zz[:, cols], e_ref[...], preferred_element_type=F32)
        y = z[:, cols] * lax.rsqrt(ss * (1.0 / head_dim) + EPS) * gain[:, cols]
        for hh in range(V7X_MXU_DIM // V7X_LANES):
            yc = y[:, hh * V7X_LANES:(hh + 1) * V7X_LANES]
            outs.append(yc * rc + pltpu.roll(yc, V7X_LANES - half, 1) * rs1
                        + pltpu.roll(yc, half, 1) * rs2)
    return jnp.concatenate(outs, axis=1)


def _proj_kernel(*refs, kind, n_tiles, group_dim, head_dim):
    h_ref, w_ref = refs[:2]
    aux, out_ref, z_ref = refs[2:-2], refs[-2], refs[-1]
    t = pl.program_id(1)

    def project(slot):
        z_ref[slot] = jnp.dot(h_ref[...], w_ref[...], preferred_element_type=F32)

    def epilogue(z):
        if kind == "fourier":
            wc_ref, = aux
            u = z.astype(BF16)
            n_groups = z.shape[1] // group_dim
            for g in range(n_groups):
                c = jnp.dot(u[:, g * group_dim:(g + 1) * group_dim], wc_ref[...],
                            preferred_element_type=F32)
                out_ref[:, g * group_dim:(g + 1) * group_dim] = c[:, :group_dim].astype(BF16)
                out_ref[:, (n_groups + g) * group_dim:(n_groups + g + 1) * group_dim] = (
                    c[:, group_dim:].astype(BF16))
        elif kind in ("q", "k"):
            e_ref, gain_ref, rc_ref, rs1_ref, rs2_ref = aux
            y = _norm_rope(z, e_ref, gain_ref[...], rc_ref[...], rs1_ref[...], rs2_ref[...],
                           head_dim)
            if kind == "q":
                out_ref[0] = y.T.astype(BF16)
            else:
                out_ref[...] = y.astype(BF16)
        elif kind == "v":
            out_ref[0] = z.T.astype(BF16)
        else:
            out_ref[...] = z.astype(BF16)

    @pl.when(t == 0)
    def _():
        project(0)

    for parity in range(2):
        @pl.when(jnp.logical_and(jnp.logical_and(t > 0, t < n_tiles), t % 2 == parity))
        def _(parity=parity):
            epilogue(z_ref[1 - parity])
            project(parity)

    @pl.when(t == n_tiles)
    def _():
        epilogue(z_ref[(n_tiles - 1) % 2])


def _proj_cast_kernel(h_ref, w_ref, out_ref):
    out_ref[...] = jnp.dot(h_ref[...], w_ref[...], preferred_element_type=F32).astype(out_ref.dtype)


def _project(h, w_in, aux, *, kind, layer, first_block, n_col_blocks, batch, seq, tm, tn, group_dim,
             head_dim):
    m, d = h.shape
    n_tiles, n_s = m // tm, seq // tm
    if kind == "gate":
        return pl.pallas_call(
            _proj_cast_kernel,
            grid=(n_col_blocks, n_tiles),
            in_specs=[pl.BlockSpec((tm, d), lambda c, t: (t, 0)),
                      pl.BlockSpec((None, d, tn), lambda c, t: (layer, 0, first_block + c))],
            out_specs=pl.BlockSpec((tm, tn), lambda c, t: (t, c)),
            out_shape=jax.ShapeDtypeStruct((m, n_col_blocks * tn), BF16),
            compiler_params=_params("parallel", "parallel"),
            name="proj_gate",
        )(h, w_in)
    done = lambda t: jnp.maximum(t - 1, 0)
    const = lambda c, t: (0, 0)
    pos = lambda c, t: (done(t) % n_s, 0)
    aux_specs = {
        "fourier": [pl.BlockSpec(a.shape, const) for a in aux],
        "q": [pl.BlockSpec(a.shape, const) for a in aux[:2]]
             + [pl.BlockSpec((tm, V7X_LANES), pos) for _ in aux[2:]],
        "v": [],
        "gate": [],
    }
    aux_specs["k"] = aux_specs["q"]
    if kind == "fourier":
        out_shape, out_spec = (m, 2 * tn), pl.BlockSpec((tm, 2 * tn), lambda c, t: (done(t), 0))
    elif kind in ("q", "v"):
        out_shape = (batch, tn, seq)
        out_spec = pl.BlockSpec((1, tn, tm), lambda c, t: (done(t) // n_s, 0, done(t) % n_s))
    else:
        out_shape = (m, n_col_blocks * tn)
        out_spec = pl.BlockSpec((tm, tn), lambda c, t: (done(t), c))
    return pl.pallas_call(
        functools.partial(_proj_kernel, kind=kind, n_tiles=n_tiles, group_dim=group_dim,
                          head_dim=head_dim),
        grid=(n_col_blocks, n_tiles + 1),
        in_specs=[
            pl.BlockSpec((tm, d), lambda c, t: (jnp.minimum(t, n_tiles - 1), 0)),
            pl.BlockSpec((None, d, tn), lambda c, t: (layer, 0, first_block + c)),
        ] + aux_specs[kind],
        out_specs=out_spec,
        out_shape=jax.ShapeDtypeStruct(out_shape, BF16),
        scratch_shapes=[pltpu.VMEM((2, tm, tn), F32)],
        compiler_params=_params("arbitrary", "arbitrary"),
        name="proj_" + kind,
    )(h, w_in, *aux)


def _fold_kernel(anti_ref, pick_ref, lo_ref, hi_ref, above_ref, mid_ref, o_ref):
    w = lo_ref.shape[2] // 2
    first = jnp.where(pl.program_id(0) == 0, 1.0, 0.0)
    for b in range(lo_ref.shape[0]):
        rev = (jnp.dot(anti_ref[...], hi_ref[b], preferred_element_type=F32)
               + jnp.dot(pick_ref[...], above_ref[b], preferred_element_type=F32))
        mid = jnp.dot(pick_ref[...], mid_ref[b], preferred_element_type=F32)
        lo = lo_ref[b].astype(F32)
        o_ref[b, :, :w] = (lo[:, :w] + rev[:, :w]).astype(BF16)
        o_ref[b, :, w:] = (lo[:, w:] - rev[:, w:] + first * mid[:, :w]).astype(BF16)


def _fold(ab, *, t):
    batch, seq, two_w = ab.shape
    nb = seq // t
    sub = 16
    r = jnp.arange(t, dtype=jnp.int32)
    anti = (r[:, None] + r[None, :] == t).astype(BF16)
    pick = (r[:, None] + jnp.arange(sub, dtype=jnp.int32)[None, :] == 0).astype(BF16)
    return pl.pallas_call(
        _fold_kernel,
        grid=(nb // 2,),
        in_specs=[
            pl.BlockSpec((t, t), lambda i: (0, 0)),
            pl.BlockSpec((t, sub), lambda i: (0, 0)),
            pl.BlockSpec((batch, t, two_w), lambda i: (0, i, 0)),
            pl.BlockSpec((batch, t, two_w), lambda i: (0, nb - 1 - i, 0)),
            pl.BlockSpec((batch, sub, two_w), lambda i: (0, ((nb - i) * (t // sub)) % (seq // sub), 0)),
            pl.BlockSpec((batch, sub, two_w), lambda i: (0, seq // 2 // sub, 0)),
        ],
        out_specs=pl.BlockSpec((batch, t, two_w), lambda i: (0, i, 0)),
        out_shape=jax.ShapeDtypeStruct((batch, seq // 2, two_w), BF16),
        compiler_params=_params("parallel"),
        name="fold",
    )(anti, pick, ab, ab, ab, ab)


def _fourier_kernel(c_ref, s_ref, a_ref, b_ref, o_ref, acc_ref):
    kk = pl.program_id(1)

    @pl.when(kk == 0)
    def _():
        acc_ref[...] = jnp.zeros_like(acc_ref)

    for b in range(a_ref.shape[0]):
        acc_ref[b] += (jnp.dot(c_ref[...], a_ref[b], preferred_element_type=F32)
                       + jnp.dot(s_ref[...], b_ref[b], preferred_element_type=F32))

    @pl.when(kk == pl.num_programs(1) - 1)
    def _():
        o_ref[...] = acc_ref[...].astype(o_ref.dtype)


def _fourier(cmat, smat, folded, *, first_row, n_rows, out_rows, tm, tk):
    batch, half_seq, two_w = folded.shape
    w = two_w // 2
    row0 = first_row // tm
    return pl.pallas_call(
        _fourier_kernel,
        grid=(n_rows // tm, half_seq // tk),
        in_specs=[
            pl.BlockSpec((tm, tk), lambda i, kk: (row0 + i, kk)),
            pl.BlockSpec((tm, tk), lambda i, kk: (row0 + i, kk)),
            pl.BlockSpec((batch, tk, w), lambda i, kk: (0, kk, 0)),
            pl.BlockSpec((batch, tk, w), lambda i, kk: (0, kk, 1)),
        ],
        out_specs=pl.BlockSpec((batch, tm, w), lambda i, kk: (0, i, 0)),
        out_shape=jax.ShapeDtypeStruct((batch, out_rows, w), BF16),
        scratch_shapes=[pltpu.VMEM((batch, tm, w), F32)],
        compiler_params=_params("parallel", "arbitrary"),
        name="fourier",
    )(cmat, smat, folded, folded)


def _unfold_kernel(anti_ref, pick_ref, perm_ref, mirror_ref, above_ref, nyq_ref, o_ref):
    group = perm_ref.shape[0]
    first = jnp.where(pl.program_id(0) == 0, 1.0, 0.0)
    for b in range(mirror_ref.shape[0]):
        rev = (jnp.dot(anti_ref[...], mirror_ref[b], preferred_element_type=F32)
               + (1.0 - first) * jnp.dot(pick_ref[...], above_ref[b], preferred_element_type=F32)
               + first * jnp.dot(pick_ref[...], nyq_ref[b], preferred_element_type=F32))
        rev = rev.astype(BF16)
        for g in range(rev.shape[1] // group):
            cols = slice(g * group, (g + 1) * group)
            o_ref[b, :, cols] = jnp.dot(rev[:, cols], perm_ref[...],
                                        preferred_element_type=F32).astype(BF16)


def _unfold(f_lo, nyq, *, group_dim, t):
    batch, half_seq, w = f_lo.shape
    nb2 = half_seq // t
    sub = nyq.shape[1]
    r = jnp.arange(t, dtype=jnp.int32)
    anti = (r[:, None] + r[None, :] == t).astype(BF16)
    pick = (r[:, None] + jnp.arange(sub, dtype=jnp.int32)[None, :] == 0).astype(BF16)
    c = jnp.arange(group_dim, dtype=jnp.int32)
    perm = ((c[:, None] + c[None, :]) % group_dim == 0).astype(BF16)
    last_sub = half_seq // sub - 1
    return pl.pallas_call(
        _unfold_kernel,
        grid=(nb2,),
        in_specs=[
            pl.BlockSpec((t, t), lambda i: (0, 0)),
            pl.BlockSpec((t, sub), lambda i: (0, 0)),
            pl.BlockSpec((group_dim, group_dim), lambda i: (0, 0)),
            pl.BlockSpec((batch, t, w), lambda i: (0, nb2 - 1 - i, 0)),
            pl.BlockSpec((batch, sub, w),
                         lambda i: (0, jnp.minimum((nb2 - i) * (t // sub), last_sub), 0)),
            pl.BlockSpec((batch, sub, w), lambda i: (0, 0, 0)),
        ],
        out_specs=pl.BlockSpec((batch, t, w), lambda i: (0, i, 0)),
        out_shape=jax.ShapeDtypeStruct(f_lo.shape, f_lo.dtype),
        compiler_params=_params("parallel"),
        name="unfold",
    )(anti, pick, perm, f_lo, f_lo, nyq)


def _attn_kernel(bounded_ref, qT_ref, k_ref, vT_ref, lq1_ref, lk1_ref, lq2_ref, lk2_ref, sub_ref,
                 o_ref, m_ref, l_ref, acc_ref, *, tk, lam_init):
    qT = qT_ref[0]
    qk_dim = qT.shape[0] // 2
    tq = qT.shape[1]
    row = lax.broadcasted_iota(jnp.int32, qT.shape, 0)
    zero = jnp.zeros_like(qT)
    q2 = jnp.concatenate([jnp.where(row < qk_dim, qT, zero), jnp.where(row >= qk_dim, qT, zero)],
                         axis=1)
    n_kb = k_ref.shape[1] // tk

    l_ref[...] = jnp.zeros_like(l_ref)
    acc_ref[...] = jnp.zeros_like(acc_ref)

    def blocks(j):
        start = pl.multiple_of(j * tk, tk)
        return k_ref[0, pl.ds(start, tk), :], vT_ref[0, :, pl.ds(start, tk)]

    @pl.when(bounded_ref[0] == 1)
    def _():
        def body(j, carry):
            kb, vb = blocks(j)
            s = jnp.dot(kb, q2, preferred_element_type=F32)
            p = jnp.exp2(s)
            l_ref[...] += jnp.sum(p.reshape(tk // 8, 8, 2 * tq), axis=0)
            acc_ref[...] += jnp.dot(vb, p.astype(BF16), preferred_element_type=F32)
            return carry

        lax.fori_loop(0, n_kb, body, 0)

    @pl.when(bounded_ref[0] != 1)
    def _():
        m_ref[...] = jnp.full_like(m_ref, NEG_BIG)

        def body(j, carry):
            kb, vb = blocks(j)
            s = jnp.dot(kb, q2, preferred_element_type=F32)
            m_prev = m_ref[...]
            m_new = jnp.maximum(m_prev, jnp.max(s, axis=0, keepdims=True))
            alpha = jnp.exp2(m_prev - m_new)
            p = jnp.exp2(s - m_new)
            l_ref[...] = alpha * l_ref[...] + jnp.sum(p.reshape(tk // 8, 8, 2 * tq), axis=0)
            acc_ref[...] = acc_ref[...] * alpha + jnp.dot(vb, p.astype(BF16),
                                                          preferred_element_type=F32)
            m_ref[...] = m_new
            return carry

        lax.fori_loop(0, n_kb, body, 0)

    lam = (jnp.exp(jnp.sum(lq1_ref[...] * lk1_ref[...], axis=-1, keepdims=True))
           - jnp.exp(jnp.sum(lq2_ref[...] * lk2_ref[...], axis=-1, keepdims=True)) + lam_init)
    l = jnp.sum(l_ref[...], axis=0, keepdims=True)
    o = acc_ref[:, :tq] * (1.0 / l[:, :tq]) - acc_ref[:, tq:] * (lam / l[:, tq:])
    scale = lax.rsqrt(jnp.mean(o * o, axis=0, keepdims=True) + EPS)
    y = o * scale * (sub_ref[...] * (1.0 - lam_init))
    o_ref[0] = y.T.astype(o_ref.dtype)


def _attention(bounded, qT, k, vT, lq1, lk1, lq2, lk2, sub, *, lam_init, n_heads, tq, tk):
    batch, d_qk, seq = qT.shape
    d_v = vT.shape[1]
    hq, hv = d_qk // n_heads, d_v // n_heads
    vec = lambda b, h, i, flag: (0, 0)
    return pl.pallas_call(
        functools.partial(_attn_kernel, tk=tk, lam_init=lam_init),
        grid_spec=pltpu.PrefetchScalarGridSpec(
            num_scalar_prefetch=1,
            grid=(batch, n_heads, seq // tq),
            in_specs=[
                pl.BlockSpec((1, hq, tq), lambda b, h, i, flag: (b, h, i)),
                pl.BlockSpec((1, seq, hq), lambda b, h, i, flag: (b, 0, h)),
                pl.BlockSpec((1, hv, seq), lambda b, h, i, flag: (b, h, 0)),
                pl.BlockSpec(lq1.shape, vec),
                pl.BlockSpec(lk1.shape, vec),
                pl.BlockSpec(lq2.shape, vec),
                pl.BlockSpec(lk2.shape, vec),
                pl.BlockSpec(sub.shape, vec),
            ],
            out_specs=pl.BlockSpec((1, tq, hv), lambda b, h, i, flag: (b, i, h)),
            scratch_shapes=[
                pltpu.VMEM((1, 2 * tq), F32),
                pltpu.VMEM((8, 2 * tq), F32),
                pltpu.VMEM((hv, 2 * tq), F32),
            ],
        ),
        out_shape=jax.ShapeDtypeStruct((batch, seq, d_v), BF16),
        compiler_params=_params("parallel", "parallel", "arbitrary"),
        name="attn",
    )(bounded, qT, k, vT, lq1, lk1, lq2, lk2, sub)


def _merge_kernel(x_ref, flo_ref, fhi_ref, o_ref, sg_ref, pf_ref, pa_ref, wo_ref, out_ref, *,
                  tiles_per_seq):
    d = x_ref.shape[1]
    upper = pl.program_id(0) % tiles_per_seq >= tiles_per_seq // 2
    f = jnp.where(upper, fhi_ref[0], flo_ref[0])
    bf = jnp.dot(f, pf_ref[...], preferred_element_type=F32)
    ba = jnp.dot(o_ref[...], pa_ref[...], preferred_element_type=F32)
    mix = (jax.nn.sigmoid(sg_ref[:, :d].astype(F32)) * bf
           + jax.nn.sigmoid(sg_ref[:, d:].astype(F32)) * ba)
    out_ref[...] = x_ref[...] + jnp.dot(mix.astype(BF16), wo_ref[...], preferred_element_type=F32)


def _merge(x, f_lo, f_hi, o, sg, p_f, p_a, w_o, *, layer, tm):
    m, d = x.shape
    half_tiles = f_lo.shape[1] // tm
    n_s = 2 * half_tiles
    w = f_lo.shape[2]
    row = lambda i: (i, 0)
    stacked = lambda w: pl.BlockSpec((None,) + w.shape[1:], lambda i: (layer, 0, 0),
                                     pipeline_mode=pl.Buffered(1))
    return pl.pallas_call(
        functools.partial(_merge_kernel, tiles_per_seq=n_s),
        grid=(m // tm,),
        in_specs=[
            pl.BlockSpec((tm, d), row),
            pl.BlockSpec((1, tm, w), lambda i: (i // n_s, jnp.minimum(i % n_s, half_tiles - 1), 0)),
            pl.BlockSpec((1, tm, w), lambda i: (i // n_s, jnp.maximum(i % n_s - half_tiles, 0), 0)),
            pl.BlockSpec((tm, o.shape[1]), row),
            pl.BlockSpec((tm, sg.shape[1]), row),
            stacked(p_f),
            stacked(p_a),
            stacked(w_o),
        ],
        out_specs=pl.BlockSpec((tm, d), row),
        out_shape=jax.ShapeDtypeStruct((m, d), F32),
        compiler_params=_params("parallel"),
        name="merge",
    )(x, f_lo, f_hi, o, sg, p_f, p_a, w_o)


def _dft_tables(seq, group_dim):
    n_rows = seq // 2 + V7X_LANES
    s = jnp.arange(seq // 2, dtype=jnp.int32)[None, :]
    hi = jnp.arange(n_rows // V7X_LANES, dtype=jnp.int32)[:, None]
    lo = jnp.arange(V7X_LANES, dtype=jnp.int32)[:, None]
    ang_hi = ((hi * V7X_LANES * s) % seq).astype(F32) * (2.0 * math.pi / seq)
    ang_lo = ((lo * s) % seq).astype(F32) * (2.0 * math.pi / seq)
    ca, sa = jnp.cos(ang_hi)[:, None, :], jnp.sin(ang_hi)[:, None, :]
    cb, sb = jnp.cos(ang_lo)[None, :, :], jnp.sin(ang_lo)[None, :, :]
    k_par = (1 - 2 * (jnp.arange(n_rows, dtype=jnp.int32) % 2)).astype(F32)[:, None]
    col0 = s == 0
    cmat = jnp.where(col0, 0.5, (ca * cb - sa * sb).reshape(n_rows, seq // 2)).astype(BF16)
    smat = jnp.where(col0, k_par, (-(sa * cb + ca * sb)).reshape(n_rows, seq // 2)).astype(BF16)
    c = jnp.arange(group_dim, dtype=jnp.int32)
    angc = ((c[:, None] * c[None, :]) % group_dim).astype(F32) * (2.0 * math.pi / group_dim)
    ortho = 1.0 / math.sqrt(seq * group_dim)
    wc = (jnp.concatenate([jnp.cos(angc), jnp.sin(angc)], axis=1) * ortho).astype(BF16)
    return cmat, smat, wc


def _rope_lane_tables(seq, head_dim):
    rope_dim = head_dim // 4
    half = rope_dim // 2
    pos = jnp.arange(seq, dtype=F32)
    inv_freq = ROPE_THETA ** (-jnp.arange(0, rope_dim, 2, dtype=F32) / rope_dim)
    ang = pos[:, None] * inv_freq[None, :]
    cos, sin = jnp.cos(ang), jnp.sin(ang)
    ones = jnp.ones((seq, head_dim - rope_dim), F32)
    zeros_h = jnp.zeros((seq, half), F32)
    zeros_r = jnp.zeros((seq, head_dim - rope_dim), F32)
    rc = jnp.concatenate([cos, cos, ones], axis=1)
    rs1 = jnp.concatenate([-sin, zeros_h, zeros_r], axis=1)
    rs2 = jnp.concatenate([zeros_h, sin, zeros_r], axis=1)
    reps = V7X_LANES // head_dim
    return tuple(jnp.tile(t, (1, reps)) for t in (rc, rs1, rs2))


def kernel(x, norm_ffa, ffa_gate, ffa_up, ffa_down, norm_mix, w_in, q_norm, k_norm, lambda_q1,
           lambda_k1, lambda_q2, lambda_k2, subln, p_f, p_a, w_o, norm_ffb, ffb_gate, ffb_up,
           ffb_down, norm_out):
    batch, seq, d = x.shape
    depth = w_in.shape[0]
    head_dim = q_norm.shape[1]
    v_dim = subln.shape[1]
    d_f = p_f.shape[1]
    d_v = p_a.shape[1]
    n_heads = d_v // v_dim
    group_dim = d_f // N_FOURIER_GROUPS
    tn = d_f
    assert w_in.shape[2] == 4 * tn + 2 * d and d_v == tn and n_heads * 2 * head_dim == tn
    assert 2 * head_dim == V7X_LANES and v_dim == V7X_LANES

    m = batch * seq
    tm_ffn_a = min(1024, seq)
    tm_ffn_b = min(1024, seq)
    tm_in = min(1024, seq)
    tm_merge = min(512, seq // 2)
    t_dft = min(1024, seq)
    t_fold = min(512, seq // 2)
    tq = min(2048, seq)
    tk = min(2048, seq)
    tf = 512

    cmat, smat, wc = _dft_tables(seq, group_dim)
    rc, rs1, rs2 = _rope_lane_tables(seq, head_dim)
    blk = jnp.arange(V7X_MXU_DIM, dtype=jnp.int32) // head_dim
    e = (blk[:, None] == blk[None, :]).astype(BF16)
    q_scale = head_dim ** -0.5 * LOG2E

    ffa_gate, ffa_up, ffa_down, w_in, p_f, p_a, w_o, ffb_gate, ffb_up, ffb_down = (
        w.astype(BF16) for w in (ffa_gate, ffa_up, ffa_down, w_in, p_f, p_a, w_o, ffb_gate, ffb_up,
                                 ffb_down))
    xs = x.reshape(m, d)
    for i in range(depth):
        lam_init = 0.8 - 0.6 * math.exp(-0.3 * i)
        xs, h = _ffn(xs, norm_ffa[i][None], ffa_gate, ffa_up, ffa_down, norm_mix[i][None], layer=i,
                     tail="norm_copy", tm=tm_ffn_a, tf=tf)
        qg = jnp.tile(q_norm[i], tn // head_dim)[None] * q_scale
        kg = jnp.tile(k_norm[i], tn // head_dim)[None]
        project = functools.partial(_project, h, w_in, layer=i, batch=batch, seq=seq, tm=tm_in,
                                    tn=tn, group_dim=group_dim, head_dim=head_dim)
        ab = project((wc,), kind="fourier", first_block=0, n_col_blocks=1)
        qT = project((e, qg, rc, rs1, rs2), kind="q", first_block=1, n_col_blocks=1)
        k = project((e, kg, rc, rs1, rs2), kind="k", first_block=2, n_col_blocks=1)
        vT = project((), kind="v", first_block=3, n_col_blocks=1)
        sg = project((), kind="gate", first_block=4, n_col_blocks=2 * d // tn)
        folded = _fold(ab.reshape(batch, seq, 2 * tn), t=t_fold)
        t_half = min(t_dft, seq // 2)
        f_lo = _fourier(cmat, smat, folded, first_row=0, n_rows=seq // 2, out_rows=seq // 2,
                        tm=t_half, tk=t_half)
        nyq = _fourier(cmat, smat, folded, first_row=seq // 2, n_rows=NYQUIST_ROWS,
                       out_rows=NYQUIST_ROWS, tm=NYQUIST_ROWS, tk=t_half)
        f_hi = _unfold(f_lo, nyq, group_dim=group_dim, t=t_fold)
        score_bound = head_dim * q_scale * jnp.max(jnp.abs(q_norm[i])) * jnp.max(jnp.abs(k_norm[i]))
        bounded = (score_bound <= SCORE_BOUND_LIMIT).astype(jnp.int32).reshape(1)
        o = _attention(bounded, qT, k.reshape(batch, seq, tn), vT, lambda_q1[i][None],
                       lambda_k1[i][None], lambda_q2[i][None], lambda_k2[i][None],
                       subln[i][:, None], lam_init=lam_init, n_heads=n_heads, tq=tq, tk=tk)
        xs = _merge(xs, f_lo, f_hi, o.reshape(m, d_v), sg, p_f, p_a, w_o, layer=i, tm=tm_merge)
        xs = _ffn(xs, norm_ffb[i][None], ffb_gate, ffb_up, ffb_down, norm_out[i][None], layer=i,
                  tail="norm_inplace", tm=tm_ffn_b, tf=tf)
    return xs.reshape(batch, seq, d)
```
